```python
import jax, jax.numpy as jnp
from jax import lax
import numpy as np

D_MODEL = 4096
BATCH = 4
SEQ = 4096
DEPTH = 1

CHUNK = 64
D_MIX = D_MODEL
D_ATTN = D_MIX // 2
D_REC = D_MIX - D_ATTN
ATTN_HEAD_DIM = 128
N_ATTN_HEADS = D_ATTN // ATTN_HEAD_DIM
LEFT_CHUNKS = 8
BAND = (LEFT_CHUNKS + 1) * CHUNK
MAX_REL = 256
N_REL = 2 * MAX_REL + 1
REC_BLOCKS = 16
REC_BLOCK_DIM = D_REC // REC_BLOCKS
CONV_WIDTH = 4
RG_C = 8.0
PEER_HEADS = 8
PEER_N_KEYS = 128
PEER_N_EXPERTS = PEER_N_KEYS * PEER_N_KEYS
PEER_QDIM = 256
PEER_HALF = PEER_QDIM // 2
PEER_TOPK = 16
PEER_TOKEN_BLOCK = 64
EPS = 1e-6
IN_SPLITS = (D_ATTN, 2 * D_ATTN, 3 * D_ATTN, 3 * D_ATTN + D_REC)
IN_COLS = 3 * D_ATTN + 2 * D_REC

kernel_name = 'hymba_chunk_streaming_rglru_peer'


def rms_norm(x, g):
    xf = x.astype(jnp.float32)
    y = xf * lax.rsqrt(jnp.mean(xf * xf, axis=-1, keepdims=True) + EPS)
    return (y * g.astype(jnp.float32)).astype(x.dtype)


def chunked_band_attention(q, k, v, rel_bias):
    B, S, H, Dh = q.shape
    nc = S // CHUNK
    pad = LEFT_CHUNKS * CHUNK
    kp = jnp.pad(k, ((0, 0), (pad, 0), (0, 0), (0, 0)))
    vp = jnp.pad(v, ((0, 0), (pad, 0), (0, 0), (0, 0)))
    q_off = jnp.arange(CHUNK)[:, None]
    k_off = jnp.arange(BAND)[None, :] - pad
    rel = jnp.clip(q_off - k_off, -MAX_REL, MAX_REL) + MAX_REL
    bias = rel_bias[:, rel].astype(jnp.float32)
    scale = Dh ** -0.5
    neg = jnp.finfo(jnp.float32).min
    qc = q.reshape(B, nc, CHUNK, H, Dh).transpose(1, 0, 2, 3, 4)

    def one_chunk(args):
        q_blk, c = args
        start = c * CHUNK
        kb = lax.dynamic_slice_in_dim(kp, start, BAND, axis=1)
        vb = lax.dynamic_slice_in_dim(vp, start, BAND, axis=1)
        s = jnp.einsum('bqhd,bkhd->bhqk', q_blk, kb,
                       preferred_element_type=jnp.float32) * scale + bias
        valid = (start + jnp.arange(BAND)) >= pad
        s = jnp.where(valid[None, None, None, :], s, neg)
        p = jax.nn.softmax(s, axis=-1)
        return jnp.einsum('bhqk,bkhd->bqhd', p.astype(vb.dtype), vb)

    out = lax.map(one_chunk, (qc, jnp.arange(nc, dtype=jnp.int32)))
    return out.transpose(1, 0, 2, 3, 4).reshape(B, S, H * Dh)


def causal_depthwise_conv(x, w, b):
    S = x.shape[1]
    xp = jnp.pad(x, ((0, 0), (CONV_WIDTH - 1, 0), (0, 0)))
    out = b
    for tap in range(CONV_WIDTH):
        out = out + xp[:, tap:tap + S] * w[tap]
    return out


def rg_lru(x, w_a, b_a, w_x, b_x, lam):
    B, S, C = x.shape
    xb = x.reshape(B, S, REC_BLOCKS, REC_BLOCK_DIM)
    r = jax.nn.sigmoid(jnp.einsum('bsnc,ncd->bsnd', xb, w_a).reshape(B, S, C) + b_a)
    i = jax.nn.sigmoid(jnp.einsum('bsnc,ncd->bsnd', xb, w_x).reshape(B, S, C) + b_x)
    log_a = (-RG_C * r.astype(jnp.float32)) * jax.nn.softplus(-lam.astype(jnp.float32))
    a = jnp.exp(log_a)
    mult = jnp.sqrt(-jnp.expm1(2.0 * log_a))
    u = mult * (i * x).astype(jnp.float32)

    def combine(left, right):
        a1, b1 = left
        a2, b2 = right
        return a1 * a2, a2 * b1 + b2

    _, h = lax.associative_scan(combine, (a, u), axis=1)
    return h.astype(x.dtype)


def peer_ffn(h, w_pq, keys_1, keys_2, u_tab, v_tab):
    B, S, D = h.shape
    q = (h @ w_pq).reshape(B, S, PEER_HEADS, 2, PEER_HALF)
    s1 = jnp.einsum('bshd,hnd->bshn', q[..., 0, :], keys_1, preferred_element_type=jnp.float32)
    s2 = jnp.einsum('bshd,hnd->bshn', q[..., 1, :], keys_2, preferred_element_type=jnp.float32)
    t1, i1 = lax.top_k(s1, PEER_TOPK)
    t2, i2 = lax.top_k(s2, PEER_TOPK)
    cand = (t1[..., :, None] + t2[..., None, :]).reshape(B, S, PEER_HEADS, PEER_TOPK * PEER_TOPK)
    ts, ti = lax.top_k(cand, PEER_TOPK)
    ia = ti // PEER_TOPK
    ib = ti % PEER_TOPK
    expert = (jnp.take_along_axis(i1, ia, axis=-1) * PEER_N_KEYS
              + jnp.take_along_axis(i2, ib, axis=-1))
    gate = jax.nn.softmax(ts, axis=-1)
    k_all = PEER_HEADS * PEER_TOPK
    nb = (B * S) // PEER_TOKEN_BLOCK
    hb = h.reshape(nb, PEER_TOKEN_BLOCK, D)
    eb = expert.reshape(nb, PEER_TOKEN_BLOCK, k_all)
    gb = gate.reshape(nb, PEER_TOKEN_BLOCK, k_all)

    def block(args):
        hx, ex, gx = args
        u_sel = u_tab[ex]
        act = jax.nn.gelu(jnp.einsum('td,tkd->tk', hx, u_sel, preferred_element_type=jnp.float32))
        w = (gx * act).astype(hx.dtype)
        return jnp.einsum('tk,tkd->td', w, v_tab[ex])

    y = lax.map(block, (hb, eb, gb))
    return y.reshape(B, S, D)


def hybrid_layer(x, g_norm1, w_in, g_q_norm, g_k_norm, rel_bias, conv_w, conv_b,
                 w_rg_a, b_rg_a, w_rg_x, b_rg_x, rg_lambda, g_attn_out, g_rec_out,
                 w_out, g_norm2, w_peer_q, peer_keys_1, peer_keys_2, peer_u, peer_v):
    B, S, _ = x.shape
    h = rms_norm(x, g_norm1)
    proj = h @ w_in
    q, k, v, x_rec, y_rec = jnp.split(proj, IN_SPLITS, axis=-1)
    q = rms_norm(q.reshape(B, S, N_ATTN_HEADS, ATTN_HEAD_DIM), g_q_norm)
    k = rms_norm(k.reshape(B, S, N_ATTN_HEADS, ATTN_HEAD_DIM), g_k_norm)
    v = v.reshape(B, S, N_ATTN_HEADS, ATTN_HEAD_DIM)
    attn = chunked_band_attention(q, k, v, rel_bias)
    xr = causal_depthwise_conv(x_rec, conv_w, conv_b)
    rec = rg_lru(xr, w_rg_a, b_rg_a, w_rg_x, b_rg_x, rg_lambda) * jax.nn.gelu(y_rec)
    mixed = jnp.concatenate([rms_norm(attn, g_attn_out), rms_norm(rec, g_rec_out)], axis=-1)
    x = x + mixed @ w_out
    x = x + peer_ffn(rms_norm(x, g_norm2), w_peer_q, peer_keys_1, peer_keys_2, peer_u, peer_v)
    return x


def setup_inputs(seed: int = 0) -> dict:
    key = jax.random.key(seed)
    ks = jax.random.split(key, 24)
    f32 = jnp.float32
    L = DEPTH

    def nrm(k, shape, scale):
        return jax.random.normal(k, shape, f32) * scale

    def gain(k, shape):
        return 1.0 + 0.01 * jax.random.normal(k, shape, f32)

    a8 = jax.random.uniform(ks[12], (L, D_REC), f32, 0.9, 0.999)
    s = a8 ** (1.0 / RG_C)
    rg_lambda = jnp.log(s) - jnp.log1p(-s)
    return {
        'x': jax.random.normal(ks[0], (BATCH, SEQ, D_MODEL), f32),
        'g_norm1': gain(ks[1], (L, D_MODEL)),
        'w_in': nrm(ks[2], (L, D_MODEL, IN_COLS), D_MODEL ** -0.5),
        'g_q_norm': gain(ks[3], (L, ATTN_HEAD_DIM)),
        'g_k_norm': gain(ks[4], (L, ATTN_HEAD_DIM)),
        'rel_bias': nrm(ks[5], (L, N_ATTN_HEADS, N_REL), 0.1),
        'conv_w': nrm(ks[6], (L, CONV_WIDTH, D_REC), CONV_WIDTH ** -0.5),
        'conv_b': nrm(ks[7], (L, D_REC), 0.01),
        'w_rg_a': nrm(ks[8], (L, REC_BLOCKS, REC_BLOCK_DIM, REC_BLOCK_DIM), REC_BLOCK_DIM ** -0.5),
        'b_rg_a': nrm(ks[9], (L, D_REC), 0.01),
        'w_rg_x': nrm(ks[10], (L, REC_BLOCKS, REC_BLOCK_DIM, REC_BLOCK_DIM), REC_BLOCK_DIM ** -0.5),
        'b_rg_x': nrm(ks[11], (L, D_REC), 0.01),
        'rg_lambda': rg_lambda,
        'g_attn_out': gain(ks[13], (L, D_ATTN)),
        'g_rec_out': gain(ks[14], (L, D_REC)),
        'w_out': nrm(ks[15], (L, D_MIX, D_MODEL), D_MIX ** -0.5),
        'g_norm2': gain(ks[16], (L, D_MODEL)),
        'w_peer_q': nrm(ks[17], (L, D_MODEL, PEER_HEADS * PEER_QDIM), D_MODEL ** -0.5),
        'peer_keys_1': nrm(ks[18], (L, PEER_HEADS, PEER_N_KEYS, PEER_HALF), PEER_HALF ** -0.5),
        'peer_keys_2': nrm(ks[19], (L, PEER_HEADS, PEER_N_KEYS, PEER_HALF), PEER_HALF ** -0.5),
        'peer_u': nrm(ks[20], (L, PEER_N_EXPERTS, D_MODEL), D_MODEL ** -0.5),
        'peer_v': nrm(ks[21], (L, PEER_N_EXPERTS, D_MODEL), PEER_HEADS ** -0.5),
    }


def reference(x, g_norm1, w_in, g_q_norm, g_k_norm, rel_bias, conv_w, conv_b,
              w_rg_a, b_rg_a, w_rg_x, b_rg_x, rg_lambda, g_attn_out, g_rec_out,
              w_out, g_norm2, w_peer_q, peer_keys_1, peer_keys_2, peer_u, peer_v):
    for l in range(DEPTH):
        x = hybrid_layer(x, g_norm1[l], w_in[l], g_q_norm[l], g_k_norm[l], rel_bias[l],
                         conv_w[l], conv_b[l], w_rg_a[l], b_rg_a[l], w_rg_x[l], b_rg_x[l],
                         rg_lambda[l], g_attn_out[l], g_rec_out[l], w_out[l], g_norm2[l],
                         w_peer_q[l], peer_keys_1[l], peer_keys_2[l], peer_u[l], peer_v[l])
    return x
```

```python
import functools
import math

import jax
import jax.numpy as jnp
from jax import lax
from jax.experimental import pallas as pl
from jax.experimental.pallas import tpu as pltpu

EPS = 1e-6
CHUNK = 64
LEFT_CHUNKS = 8
LEFT = LEFT_CHUNKS * CHUNK
MAX_REL = 256
HEAD_DIM = 128
REC_BLOCK_DIM = 128
CONV_WIDTH = 4
RG_C = 8.0
PEER_HEADS = 8
PEER_N_KEYS = 128
PEER_TOPK = 16
LANES = 128
SUBLANES = 8
NEG = -1e30
VMEM_LIMIT_BYTES = 56 * 1024 * 1024

F32 = jnp.float32
BF16 = jnp.bfloat16
_NT = (((1,), (1,)), ((), ()))
_TN = (((0,), (0,)), ((), ()))


def _params(*sem):
    return pltpu.CompilerParams(dimension_semantics=sem, vmem_limit_bytes=VMEM_LIMIT_BYTES)


def _gelu_tanh(x):
    c = math.sqrt(2.0 / math.pi)
    return x * (0.5 * (1.0 + jnp.tanh(c * (x + 0.044715 * (x * x * x)))))


def _rmsnorm_kernel(x_ref, g_ref, o_ref):
    x = x_ref[...].astype(F32)
    ms = jnp.mean(x * x, axis=-1, keepdims=True)
    o_ref[...] = (x * lax.rsqrt(ms + EPS) * g_ref[...]).astype(o_ref.dtype)


def rmsnorm_rows(x, g, tm):
    t, d = x.shape
    return pl.pallas_call(
        _rmsnorm_kernel,
        grid=(t // tm,),
        in_specs=[pl.BlockSpec((tm, d), lambda i: (i, 0)),
                  pl.BlockSpec((1, d), lambda i: (0, 0))],
        out_specs=pl.BlockSpec((tm, d), lambda i: (i, 0)),
        out_shape=jax.ShapeDtypeStruct((t, d), BF16),
        compiler_params=_params("parallel"),
        name="rmsnorm_rows",
    )(x, g.reshape(1, d).astype(F32))


def _in_proj_kernel(a_ref, b_ref, g_ref, o_ref, *, n_norm_tiles):
    acc = jnp.dot(a_ref[...], b_ref[...], preferred_element_type=F32)
    j = pl.program_id(1)
    tn = o_ref.shape[1]

    @pl.when(j < n_norm_tiles)
    def _():
        g = g_ref[0]
        for c in range(tn // HEAD_DIM):
            cols = slice(c * HEAD_DIM, (c + 1) * HEAD_DIM)
            blk = acc[:, cols]
            ms = jnp.mean(blk * blk, axis=-1, keepdims=True)
            o_ref[:, cols] = (blk * lax.rsqrt(ms + EPS) * g).astype(o_ref.dtype)

    @pl.when(j >= n_norm_tiles)
    def _():
        o_ref[...] = acc.astype(o_ref.dtype)


def in_proj(h, w, gains, *, col0, n_cols, n_norm_cols, out_dtype, tm, tn):
    t, k = h.shape
    n_sec = gains.shape[0]
    sec_tiles = max(n_norm_cols // tn // n_sec, 1)
    jb0 = col0 // tn
    return pl.pallas_call(
        functools.partial(_in_proj_kernel, n_norm_tiles=n_norm_cols // tn),
        grid=(t // tm, n_cols // tn),
        in_specs=[pl.BlockSpec((tm, k), lambda i, j: (i, 0)),
                  pl.BlockSpec((k, tn), lambda i, j: (0, j + jb0)),
                  pl.BlockSpec((1, 1, HEAD_DIM),
                               lambda i, j: (jnp.minimum(j // sec_tiles, n_sec - 1), 0, 0))],
        out_specs=pl.BlockSpec((tm, tn), lambda i, j: (i, j)),
        out_shape=jax.ShapeDtypeStruct((t, n_cols), out_dtype),
        compiler_params=_params("parallel", "arbitrary"),
        name="in_proj",
    )(h, w, gains.reshape(n_sec, 1, HEAD_DIM).astype(F32))


def _attn_bias_table(rel_bias, tq):
    r = jnp.arange(tq)[:, None]
    kpos = jnp.arange(LEFT + tq)[None, :] - LEFT
    rel = jnp.clip(r - kpos, -MAX_REL, MAX_REL) + MAX_REL
    dchunk = r // CHUNK - jnp.floor_divide(kpos, CHUNK)
    band = (dchunk >= 0) & (dchunk <= LEFT_CHUNKS)
    return jnp.where(band[None], rel_bias.astype(F32)[:, rel], NEG)


def _attn_kernel(*refs, n_prev, blocks_per_seq, scale):
    q_ref = refs[0]
    k_refs = refs[1:2 + n_prev]
    v_refs = refs[2 + n_prev:3 + 2 * n_prev]
    tbl_ref, o_ref = refs[3 + 2 * n_prev:]
    tq = q_ref.shape[0]
    jb = pl.program_id(1) % blocks_per_seq
    k = jnp.concatenate([r[...] for r in k_refs], axis=0)
    v = jnp.concatenate([r[...] for r in v_refs], axis=0)
    s = lax.dot_general(q_ref[...], k, _NT, preferred_element_type=F32)
    s = s * scale + tbl_ref[0]
    col = lax.broadcasted_iota(jnp.int32, s.shape, 1)
    s = jnp.where(col >= jnp.maximum(n_prev - jb, 0) * tq, s, NEG)
    m = jnp.max(s, axis=-1, keepdims=True)
    p = jnp.exp(s - m)
    l = jnp.sum(p, axis=-1, keepdims=True)
    o = jnp.dot(p.astype(v.dtype), v, preferred_element_type=F32)
    o_ref[...] = (o / l).astype(o_ref.dtype)


def band_attention(qkv, rel_bias, *, seq, n_heads, tq):
    t = qkv.shape[0]
    n_prev = LEFT // tq
    bps = seq // tq
    tbl = _attn_bias_table(rel_bias, tq)

    def win_spec(col_base, back):
        def idx(h, i):
            first = (i // bps) * bps
            return (jnp.maximum(i - back, first), col_base + h)
        return pl.BlockSpec((tq, HEAD_DIM), idx)

    backs = list(range(n_prev, -1, -1))
    in_specs = ([pl.BlockSpec((tq, HEAD_DIM), lambda h, i: (i, h))]
                + [win_spec(n_heads, b) for b in backs]
                + [win_spec(2 * n_heads, b) for b in backs]
                + [pl.BlockSpec((1, tq, LEFT + tq), lambda h, i: (h, 0, 0))])
    n_win = n_prev + 1
    return pl.pallas_call(
        functools.partial(_attn_kernel, n_prev=n_prev, blocks_per_seq=bps,
                          scale=HEAD_DIM ** -0.5),
        grid=(n_heads, t // tq),
        in_specs=in_specs,
        out_specs=pl.BlockSpec((tq, HEAD_DIM), lambda h, i: (i, h)),
        out_shape=jax.ShapeDtypeStruct((t, n_heads * HEAD_DIM), F32),
        compiler_params=_params("parallel", "arbitrary"),
        name="band_attention",
    )(*([qkv] * (1 + 2 * n_win)), tbl)


def _rec_kernel(x_ref, y_ref, cw_ref, cb_ref, wa_ref, wx_ref, ba_ref, bx_ref, lam_ref, g_ref,
                o_ref, xpad, hcar, rec_scr, *, lane_chunk):
    tt, c_all = x_ref.shape
    groups = tt // SUBLANES
    pad0 = SUBLANES - (CONV_WIDTH - 1)

    @pl.when(pl.program_id(1) == 0)
    def _():
        xpad[0:SUBLANES, :] = jnp.zeros((SUBLANES, c_all), F32)
        hcar[...] = jnp.zeros_like(hcar)

    xpad[SUBLANES:SUBLANES + tt, :] = x_ref[...]
    ss = jnp.zeros((tt, 1), F32)
    row = lax.broadcasted_iota(jnp.int32, (groups, SUBLANES, lane_chunk), 1)
    for cbk in range(c_all // lane_chunk):
        ls = slice(cbk * lane_chunk, (cbk + 1) * lane_chunk)
        xr = cb_ref[:, ls]
        for tap in range(CONV_WIDTH):
            xr = xr + xpad[pad0 + tap:pad0 + tap + tt, ls] * cw_ref[tap:tap + 1, ls]
        xr_b = xr.astype(BF16)
        pre_a, pre_x = [], []
        for n in range(lane_chunk // REC_BLOCK_DIM):
            nb = cbk * (lane_chunk // REC_BLOCK_DIM) + n
            xs = xr_b[:, n * REC_BLOCK_DIM:(n + 1) * REC_BLOCK_DIM]
            pre_a.append(jnp.dot(xs, wa_ref[nb], preferred_element_type=F32))
            pre_x.append(jnp.dot(xs, wx_ref[nb], preferred_element_type=F32))
        r = jax.nn.sigmoid(jnp.concatenate(pre_a, axis=1) + ba_ref[:, ls])
        gate_i = jax.nn.sigmoid(jnp.concatenate(pre_x, axis=1) + bx_ref[:, ls])
        z = -lam_ref[:, ls]
        softplus = jnp.maximum(z, 0.0) + jnp.log1p(jnp.exp(-jnp.abs(z)))
        log_a = (-RG_C * r) * softplus
        a = jnp.exp(log_a)
        mult = jnp.sqrt(-jnp.tanh(log_a) * (a * a + 1.0))
        u = mult * (gate_i * xr)
        a3 = a.reshape(groups, SUBLANES, lane_chunk)
        u3 = u.reshape(groups, SUBLANES, lane_chunk)
        d = 1
        while d < SUBLANES:
            keep = row >= d
            a_sh = jnp.where(keep, pltpu.roll(a3, d, axis=1), 1.0)
            u_sh = jnp.where(keep, pltpu.roll(u3, d, axis=1), 0.0)
            u3 = a3 * u_sh + u3
            a3 = a3 * a_sh
            d *= 2
        h = hcar[:, ls]
        hs = []
        for gi in range(groups):
            hg = u3[gi] + a3[gi] * h
            hs.append(hg)
            h = hg[SUBLANES - 1:SUBLANES, :]
        hcar[:, ls] = h
        rec = jnp.concatenate(hs, axis=0) * _gelu_tanh(y_ref[:, ls])
        rec_scr[:, ls] = rec
        ss = ss + jnp.sum(rec * rec, axis=-1, keepdims=True)
    xpad[0:SUBLANES, :] = x_ref[tt - SUBLANES:tt, :]
    o_ref[...] = (rec_scr[...] * lax.rsqrt(ss / c_all + EPS) * g_ref[...]).astype(o_ref.dtype)


def rec_branch(xy, conv_w, conv_b, w_a, b_a, w_x, b_x, lam, g_out, *, batch, tt, lane_chunk):
    t, c2 = xy.shape
    c = c2 // 2
    nt = t // batch // tt
    row = lambda v: v.reshape(1, c).astype(F32)
    vec = pl.BlockSpec((1, c), lambda b, i: (0, 0))
    wspec = pl.BlockSpec(w_a.shape, lambda b, i: (0, 0, 0))
    return pl.pallas_call(
        functools.partial(_rec_kernel, lane_chunk=lane_chunk),
        grid=(batch, nt),
        in_specs=[pl.BlockSpec((tt, c), lambda b, i: (b * nt + i, 0)),
                  pl.BlockSpec((tt, c), lambda b, i: (b * nt + i, 1)),
                  pl.BlockSpec((CONV_WIDTH, c), lambda b, i: (0, 0)),
                  vec, wspec, wspec, vec, vec, vec, vec],
        out_specs=pl.BlockSpec((tt, c), lambda b, i: (b * nt + i, 0)),
        out_shape=jax.ShapeDtypeStruct((t, c), BF16),
        scratch_shapes=[pltpu.VMEM((SUBLANES + tt, c), F32),
                        pltpu.VMEM((1, c), F32),
                        pltpu.VMEM((tt, c), F32)],
        compiler_params=_params("parallel", "arbitrary"),
        name="rec_branch",
    )(xy, xy, conv_w.astype(F32), row(conv_b), w_a.astype(BF16), w_x.astype(BF16),
      row(b_a), row(b_x), row(lam), row(g_out))


def _out_proj_kernel(a1_ref, a2_ref, b1_ref, b2_ref, x_ref, o_ref):
    acc = jnp.dot(a1_ref[...], b1_ref[...], preferred_element_type=F32)
    acc = acc + jnp.dot(a2_ref[...], b2_ref[...], preferred_element_type=F32)
    o_ref[...] = x_ref[...] + acc


def out_proj(a1, a2, w, x, *, tm, tn):
    t, k1 = a1.shape
    n = w.shape[1]
    return pl.pallas_call(
        _out_proj_kernel,
        grid=(t // tm, n // tn),
        in_specs=[pl.BlockSpec((tm, k1), lambda i, j: (i, 0)),
                  pl.BlockSpec((tm, k1), lambda i, j: (i, 0)),
                  pl.BlockSpec((k1, tn), lambda i, j: (0, j)),
                  pl.BlockSpec((k1, tn), lambda i, j: (1, j)),
                  pl.BlockSpec((tm, tn), lambda i, j: (i, j))],
        out_specs=pl.BlockSpec((tm, tn), lambda i, j: (i, j)),
        out_shape=jax.ShapeDtypeStruct((t, n), F32),
        compiler_params=_params("parallel", "arbitrary"),
        name="out_proj",
    )(a1, a2, w, w, x)


def _peer_scores_kernel(a_ref, b_ref, keys_ref, st_ref):
    q = jnp.dot(a_ref[...], b_ref[...], preferred_element_type=F32).astype(BF16)
    for c in range(keys_ref.shape[0]):
        rows = slice(c * PEER_N_KEYS, (c + 1) * PEER_N_KEYS)
        st_ref[rows, :] = lax.dot_general(keys_ref[c], q[:, rows], _NT,
                                          preferred_element_type=F32)


def peer_scores(h, w_q, keys, *, tm, tn):
    t, k = h.shape
    n = w_q.shape[1]
    kb = tn // PEER_N_KEYS
    return pl.pallas_call(
        _peer_scores_kernel,
        grid=(t // tm, n // tn),
        in_specs=[pl.BlockSpec((tm, k), lambda i, j: (i, 0)),
                  pl.BlockSpec((k, tn), lambda i, j: (0, j)),
                  pl.BlockSpec((kb,) + keys.shape[1:], lambda i, j: (j, 0, 0))],
        out_specs=pl.BlockSpec((tn, tm), lambda i, j: (j, i)),
        out_shape=jax.ShapeDtypeStruct((n, t), F32),
        compiler_params=_params("parallel", "arbitrary"),
        name="peer_scores",
    )(h, w_q, keys)


_CAND_PAIRS = [(a, b) for a in range(PEER_TOPK) for b in range(PEER_TOPK)
               if (a + 1) * (b + 1) <= PEER_TOPK]


def _top_values(s, count):
    rows = []
    for _ in range(count):
        m = jnp.max(s, axis=0, keepdims=True)
        rows.append(m)
        s = jnp.where(s == m, NEG, s)
    return rows


def _peer_topk_kernel(s_ref, tau_ref, e2_ref, s1t_ref, ct_ref, c_scr):
    nh, _, nk, _ = s_ref.shape

    def head(h, carry):
        s1 = s_ref[h, 0]
        s2 = s_ref[h, 1]
        t1 = _top_values(s1, PEER_TOPK)
        t2 = _top_values(s2, PEER_TOPK)
        cands = jnp.concatenate([t1[a] + t2[b] for a, b in _CAND_PAIRS], axis=0)
        tau = _top_values(cands, PEER_TOPK)[-1]
        z = jnp.sum(jnp.where(cands >= tau, jnp.exp(cands - (t1[0] + t2[0])), 0.0),
                    axis=0, keepdims=True)
        tau_ref[pl.ds(h, 1), :] = tau
        c_scr[h] = jnp.exp(s1 - t1[0]) / z
        e2_ref[h] = jnp.exp(s2 - t2[0])
        return carry

    lax.fori_loop(0, nh, head, 0)
    for h in range(nh):
        for lc in range(s1t_ref.shape[0]):
            ls = slice(lc * LANES, (lc + 1) * LANES)
            s1t_ref[lc, pl.ds(h, nk, stride=nh), :] = s_ref[h, 0, :, ls]
            ct_ref[lc, pl.ds(h, nk, stride=nh), :] = c_scr[h, :, ls]


def peer_topk(st, *, tt):
    nh, _, nk, t = st.shape
    return pl.pallas_call(
        _peer_topk_kernel,
        grid=(t // tt,),
        in_specs=[pl.BlockSpec((nh, 2, nk, tt), lambda i: (0, 0, 0, i))],
        out_specs=[pl.BlockSpec((nh, tt), lambda i: (0, i)),
                   pl.BlockSpec((nh, nk, tt), lambda i: (0, 0, i)),
                   pl.BlockSpec((tt // LANES, nk * nh, LANES), lambda i: (i, 0, 0)),
                   pl.BlockSpec((tt // LANES, nk * nh, LANES), lambda i: (i, 0, 0))],
        out_shape=[jax.ShapeDtypeStruct((nh, t), F32),
                   jax.ShapeDtypeStruct((nh, nk, t), F32),
                   jax.ShapeDtypeStruct((t // LANES, nk * nh, LANES), F32),
                   jax.ShapeDtypeStruct((t // LANES, nk * nh, LANES), F32)],
        scratch_shapes=[pltpu.VMEM((nh, nk, tt), F32)],
        compiler_params=_params("parallel"),
        name="peer_topk",
    )(st)


def _peer_dense_kernel(h_ref, u_ref, v_ref, s2_ref, e2_ref, tau_ref, s1t_ref, ct_ref, x_ref,
                       o_ref, act_scr, w_scr):
    te, tt = act_scr.shape
    nh, nk = s2_ref.shape[0], s2_ref.shape[1]

    @pl.when(pl.program_id(1) == 0)
    def _():
        o_ref[...] = x_ref[...]

    act_scr[...] = lax.dot_general(u_ref[...], h_ref[...], _NT, preferred_element_type=F32)
    for ii in range(te // nk):
        rows = slice(ii * nk, (ii + 1) * nk)
        for lc in range(tt // LANES):
            ls = slice(lc * LANES, (lc + 1) * LANES)
            gate = jnp.zeros((nk, LANES), F32)
            for h in range(nh):
                r = ii * nh + h
                pair_sum = s2_ref[h, :, ls] + s1t_ref[lc, r:r + 1, :]
                weight = e2_ref[h, :, ls] * ct_ref[lc, r:r + 1, :]
                gate = gate + jnp.where(pair_sum >= tau_ref[h:h + 1, ls], weight, 0.0)
            w_scr[rows, ls] = (gate * _gelu_tanh(act_scr[rows, ls])).astype(w_scr.dtype)
    o_ref[...] += lax.dot_general(w_scr[...], v_ref[...], _TN, preferred_element_type=F32)


def peer_dense(h, u, v, st, tau, e2, s1t, ct, x, *, tt, te):
    t, d = h.shape
    e = u.shape[0]
    nh, _, nk, _ = st.shape
    once = pl.Buffered(1)
    key_rows = te // nk * nh
    return pl.pallas_call(
        _peer_dense_kernel,
        grid=(t // tt, e // te),
        in_specs=[pl.BlockSpec((tt, d), lambda i, j: (i, 0), pipeline_mode=once),
                  pl.BlockSpec((te, d), lambda i, j: (j, 0)),
                  pl.BlockSpec((te, d), lambda i, j: (j, 0)),
                  pl.BlockSpec((nh, None, nk, tt), lambda i, j: (0, 1, 0, i), pipeline_mode=once),
                  pl.BlockSpec((nh, nk, tt), lambda i, j: (0, 0, i), pipeline_mode=once),
                  pl.BlockSpec((nh, tt), lambda i, j: (0, i), pipeline_mode=once),
                  pl.BlockSpec((tt // LANES, key_rows, LANES), lambda i, j: (i, j, 0)),
                  pl.BlockSpec((tt // LANES, key_rows, LANES), lambda i, j: (i, j, 0)),
                  pl.BlockSpec((tt, d), lambda i, j: (i, 0), pipeline_mode=once)],
        out_specs=pl.BlockSpec((tt, d), lambda i, j: (i, 0)),
        out_shape=jax.ShapeDtypeStruct((t, d), F32),
        scratch_shapes=[pltpu.VMEM((te, tt), F32), pltpu.VMEM((te, tt), BF16)],
        compiler_params=_params("parallel", "arbitrary"),
        name="peer_dense",
    )(h, u, v, st, e2, tau, s1t, ct, x)


def hybrid_layer(x, g_norm1, w_in, g_q_norm, g_k_norm, rel_bias, conv_w, conv_b,
                 w_rg_a, b_rg_a, w_rg_x, b_rg_x, rg_lambda, g_attn_out, g_rec_out,
                 w_out, g_norm2, w_peer_q, peer_keys_1, peer_keys_2, peer_u, peer_v,
                 *, cfg):
    b, s, d = x.shape
    t = b * s
    d_attn = g_attn_out.shape[0]
    d_rec = g_rec_out.shape[0]
    n_heads = d_attn // HEAD_DIM
    x2 = x.reshape(t, d)

    h1 = rmsnorm_rows(x2, g_norm1, cfg["norm_tm"])
    w_in_b = w_in.astype(BF16)
    qkv = in_proj(h1, w_in_b, jnp.stack([g_q_norm, g_k_norm]), col0=0, n_cols=3 * d_attn,
                  n_norm_cols=2 * d_attn, out_dtype=BF16, tm=cfg["mm_tm"], tn=cfg["mm_tn"])
    xy = in_proj(h1, w_in_b, jnp.ones((1, HEAD_DIM), F32), col0=3 * d_attn, n_cols=2 * d_rec,
                 n_norm_cols=0, out_dtype=F32, tm=cfg["mm_tm"], tn=cfg["mm_tn"])

    attn = band_attention(qkv, rel_bias, seq=s, n_heads=n_heads, tq=cfg["attn_tq"])
    attn_n = rmsnorm_rows(attn, g_attn_out, cfg["norm_tm"])
    rec_n = rec_branch(xy, conv_w, conv_b, w_rg_a, b_rg_a, w_rg_x, b_rg_x, rg_lambda, g_rec_out,
                       batch=b, tt=cfg["rec_tt"], lane_chunk=cfg["rec_lanes"])
    x1 = out_proj(attn_n, rec_n, w_out.astype(BF16), x2, tm=cfg["mm_tm"], tn=cfg["mm_tn"])

    h2 = rmsnorm_rows(x1, g_norm2, cfg["norm_tm"])
    nh, nk, half = peer_keys_1.shape
    keys = jnp.stack([peer_keys_1, peer_keys_2], axis=1).reshape(2 * nh, nk, half).astype(BF16)
    st = peer_scores(h2, w_peer_q.astype(BF16), keys, tm=cfg["mm_tm"], tn=cfg["mm_tn"])
    st = st.reshape(nh, 2, nk, t)
    tau, e2, s1t, ct = peer_topk(st, tt=cfg["topk_tt"])
    out = peer_dense(h2, peer_u.astype(BF16), peer_v.astype(BF16), st, tau, e2, s1t, ct, x1,
                     tt=cfg["peer_tt"], te=cfg["peer_te"])
    return out.reshape(b, s, d)


_CFG = dict(norm_tm=256, mm_tm=1024, mm_tn=1024, attn_tq=256, rec_tt=256, rec_lanes=512,
            topk_tt=256, peer_tt=512, peer_te=512)


def kernel(x, g_norm1, w_in, g_q_norm, g_k_norm, rel_bias, conv_w, conv_b, w_rg_a, b_rg_a,
           w_rg_x, b_rg_x, rg_lambda, g_attn_out, g_rec_out, w_out, g_norm2, w_peer_q,
           peer_keys_1, peer_keys_2, peer_u, peer_v):
    for l in range(g_norm1.shape[0]):
        x = hybrid_layer(x, g_norm1[l], w_in[l], g_q_norm[l], g_k_norm[l], rel_bias[l],
                         conv_w[l], conv_b[l], w_rg_a[l], b_rg_a[l], w_rg_x[l], b_rg_x[l],
                         rg_lambda[l], g_attn_out[l], g_rec_out[l], w_out[l], g_norm2[l],
                         w_peer_q[l], peer_keys_1[l], peer_keys_2[l], peer_u[l], peer_v[l],
                         cfg=_CFG)
    return x
```

```python
import functools
import math

import jax
import jax.numpy as jnp
from jax import lax
from jax.experimental import pallas as pl
from jax.experimental.pallas import tpu as pltpu

EPS = 1e-6
CHUNK = 64
LEFT_CHUNKS = 8
LEFT = LEFT_CHUNKS * CHUNK
MAX_REL = 256
HEAD_DIM = 128
REC_BLOCK_DIM = 128
CONV_WIDTH = 4
RG_C = 8.0
PEER_HEADS = 8
PEER_N_KEYS = 128
PEER_TOPK = 16
LANES = 128
SUBLANES = 8
NEG = -1e30
VMEM_LIMIT_BYTES = 56 * 1024 * 1024

F32 = jnp.float32
BF16 = jnp.bfloat16
_NT = (((1,), (1,)), ((), ()))


def _params(*sem):
    return pltpu.CompilerParams(dimension_semantics=sem, vmem_limit_bytes=VMEM_LIMIT_BYTES)


def _gelu_tanh(x):
    c = math.sqrt(2.0 / math.pi)
    return x * (0.5 * (1.0 + jnp.tanh(c * (x + 0.044715 * (x * x * x)))))


def _rmsnorm_kernel(x_ref, g_ref, o_ref):
    x = x_ref[...].astype(F32)
    ms = jnp.mean(x * x, axis=-1, keepdims=True)
    o_ref[...] = (x * lax.rsqrt(ms + EPS) * g_ref[...]).astype(o_ref.dtype)


def rmsnorm_rows(x, g, tm):
    t, d = x.shape
    return pl.pallas_call(
        _rmsnorm_kernel,
        grid=(t // tm,),
        in_specs=[pl.BlockSpec((tm, d), lambda i: (i, 0)),
                  pl.BlockSpec((1, d), lambda i: (0, 0))],
        out_specs=pl.BlockSpec((tm, d), lambda i: (i, 0)),
        out_shape=jax.ShapeDtypeStruct((t, d), BF16),
        compiler_params=_params("parallel"),
        name="rmsnorm_rows",
    )(x, g.reshape(1, d).astype(F32))


def _in_proj_kernel(a_ref, b_ref, g_ref, o_ref, *, n_norm_tiles):
    acc = jnp.dot(a_ref[...], b_ref[...], preferred_element_type=F32)
    j = pl.program_id(1)
    tn = o_ref.shape[1]

    @pl.when(j < n_norm_tiles)
    def _():
        g = g_ref[0]
        for c in range(tn // HEAD_DIM):
            cols = slice(c * HEAD_DIM, (c + 1) * HEAD_DIM)
            blk = acc[:, cols]
            ms = jnp.mean(blk * blk, axis=-1, keepdims=True)
            o_ref[:, cols] = (blk * lax.rsqrt(ms + EPS) * g).astype(o_ref.dtype)

    @pl.when(j >= n_norm_tiles)
    def _():
        o_ref[...] = acc.astype(o_ref.dtype)


def in_proj(h, w, gains, *, col0, n_cols, n_norm_cols, out_dtype, tm, tn):
    t, k = h.shape
    n_sec = gains.shape[0]
    sec_tiles = max(n_norm_cols // tn // n_sec, 1)
    jb0 = col0 // tn
    return pl.pallas_call(
        functools.partial(_in_proj_kernel, n_norm_tiles=n_norm_cols // tn),
        grid=(t // tm, n_cols // tn),
        in_specs=[pl.BlockSpec((tm, k), lambda i, j: (i, 0)),
                  pl.BlockSpec((k, tn), lambda i, j: (0, j + jb0)),
                  pl.BlockSpec((1, 1, HEAD_DIM),
                               lambda i, j: (jnp.minimum(j // sec_tiles, n_sec - 1), 0, 0))],
        out_specs=pl.BlockSpec((tm, tn), lambda i, j: (i, j)),
        out_shape=jax.ShapeDtypeStruct((t, n_cols), out_dtype),
        compiler_params=_params("parallel", "arbitrary"),
        name="in_proj",
    )(h, w, gains.reshape(n_sec, 1, HEAD_DIM).astype(F32))


def _rel_bias_diagonals(rel_bias, tq):
    w = LEFT + tq
    length = pl.cdiv(w + tq - 1, LANES) * LANES
    m = jnp.arange(length)
    d = jnp.where(m < w, m, m - length)
    rel = jnp.clip(LEFT - d, -MAX_REL, MAX_REL) + MAX_REL
    return rel_bias.astype(F32)[:, None, rel]


def _attn_kernel(*refs, n_prev, blocks_per_seq, scale):
    q_ref = refs[0]
    k_refs = refs[1:2 + n_prev]
    v_refs = refs[2 + n_prev:3 + 2 * n_prev]
    diag_ref, o_ref, tbl_scr = refs[3 + 2 * n_prev:]
    heads, tq, w = tbl_scr.shape
    i = pl.program_id(1)

    @pl.when(i == 0)
    def _():
        r = lax.broadcasted_iota(jnp.int32, (tq, w), 0)
        c = lax.broadcasted_iota(jnp.int32, (tq, w), 1)
        dchunk = r // CHUNK - c // CHUNK + LEFT_CHUNKS
        band = (dchunk >= 0) & (dchunk <= LEFT_CHUNKS)
        for hh in range(heads):
            diag = jnp.broadcast_to(diag_ref[hh], (tq, diag_ref.shape[2]))
            bias = pltpu.roll(diag, 0, axis=1, stride=1, stride_axis=0)[:, :w]
            tbl_scr[hh] = jnp.where(band, bias, NEG)

    jb = i % blocks_per_seq
    col = lax.broadcasted_iota(jnp.int32, (tq, w), 1)
    in_seq = col >= jnp.maximum(n_prev - jb, 0) * tq
    for hh in range(heads):
        cols = slice(hh * HEAD_DIM, (hh + 1) * HEAD_DIM)
        k = jnp.concatenate([r[:, cols] for r in k_refs], axis=0)
        v = jnp.concatenate([r[:, cols] for r in v_refs], axis=0)
        s = lax.dot_general(q_ref[:, cols], k, _NT, preferred_element_type=F32)
        s = jnp.where(in_seq, s * scale + tbl_scr[hh], NEG)
        m = jnp.max(s, axis=-1, keepdims=True)
        p = jnp.exp(s - m)
        l = jnp.sum(p, axis=-1, keepdims=True)
        o = jnp.dot(p.astype(v.dtype), v, preferred_element_type=F32)
        o_ref[:, cols] = (o / l).astype(o_ref.dtype)


def band_attention(qkv, rel_bias, *, seq, n_heads, tq, heads_per_step):
    t = qkv.shape[0]
    n_prev = LEFT // tq
    bps = seq // tq
    hp = heads_per_step
    groups = n_heads // hp
    diags = _rel_bias_diagonals(rel_bias, tq)

    def win_spec(col_base, back):
        def idx(g, i):
            first = (i // bps) * bps
            return (jnp.maximum(i - back, first), col_base + g)
        return pl.BlockSpec((tq, hp * HEAD_DIM), idx)

    backs = list(range(n_prev, -1, -1))
    in_specs = ([pl.BlockSpec((tq, hp * HEAD_DIM), lambda g, i: (i, g))]
                + [win_spec(groups, b) for b in backs]
                + [win_spec(2 * groups, b) for b in backs]
                + [pl.BlockSpec((hp, 1, diags.shape[2]), lambda g, i: (g, 0, 0))])
    n_win = n_prev + 1
    return pl.pallas_call(
        functools.partial(_attn_kernel, n_prev=n_prev, blocks_per_seq=bps,
                          scale=HEAD_DIM ** -0.5),
        grid=(groups, t // tq),
        in_specs=in_specs,
        out_specs=pl.BlockSpec((tq, hp * HEAD_DIM), lambda g, i: (i, g)),
        out_shape=jax.ShapeDtypeStruct((t, n_heads * HEAD_DIM), F32),
        scratch_shapes=[pltpu.VMEM((hp, tq, LEFT + tq), F32)],
        compiler_params=_params("parallel", "arbitrary"),
        name="band_attention",
    )(*([qkv] * (1 + 2 * n_win)), diags)


def _rec_kernel(x_ref, y_ref, cw_ref, cb_ref, wa_ref, wx_ref, ba_ref, bx_ref, lam_ref, g_ref,
                o_ref, xpad, hcar, rec_scr, *, lane_chunk):
    tt, c_all = x_ref.shape
    groups = tt // SUBLANES
    pad0 = SUBLANES - (CONV_WIDTH - 1)

    @pl.when(pl.program_id(1) == 0)
    def _():
        xpad[0:SUBLANES, :] = jnp.zeros((SUBLANES, c_all), F32)
        hcar[...] = jnp.zeros_like(hcar)

    xpad[SUBLANES:SUBLANES + tt, :] = x_ref[...]
    ss = jnp.zeros((tt, 1), F32)
    row = lax.broadcasted_iota(jnp.int32, (groups, SUBLANES, lane_chunk), 1)
    for cbk in range(c_all // lane_chunk):
        ls = slice(cbk * lane_chunk, (cbk + 1) * lane_chunk)
        xr = cb_ref[:, ls]
        for tap in range(CONV_WIDTH):
            xr = xr + xpad[pad0 + tap:pad0 + tap + tt, ls] * cw_ref[tap:tap + 1, ls]
        xr_b = xr.astype(BF16)
        pre_a, pre_x = [], []
        for n in range(lane_chunk // REC_BLOCK_DIM):
            nb = cbk * (lane_chunk // REC_BLOCK_DIM) + n
            xs = xr_b[:, n * REC_BLOCK_DIM:(n + 1) * REC_BLOCK_DIM]
            pre_a.append(jnp.dot(xs, wa_ref[nb], preferred_element_type=F32))
            pre_x.append(jnp.dot(xs, wx_ref[nb], preferred_element_type=F32))
        r = jax.nn.sigmoid(jnp.concatenate(pre_a, axis=1) + ba_ref[:, ls])
        gate_i = jax.nn.sigmoid(jnp.concatenate(pre_x, axis=1) + bx_ref[:, ls])
        z = -lam_ref[:, ls]
        softplus = jnp.maximum(z, 0.0) + jnp.log1p(jnp.exp(-jnp.abs(z)))
        log_a = (-RG_C * r) * softplus
        a = jnp.exp(log_a)
        mult = jnp.sqrt(-jnp.tanh(log_a) * (a * a + 1.0))
        u = mult * (gate_i * xr)
        a3 = a.reshape(groups, SUBLANES, lane_chunk)
        u3 = u.reshape(groups, SUBLANES, lane_chunk)
        d = 1
        while d < SUBLANES:
            keep = row >= d
            a_sh = jnp.where(keep, pltpu.roll(a3, d, axis=1), 1.0)
            u_sh = jnp.where(keep, pltpu.roll(u3, d, axis=1), 0.0)
            u3 = a3 * u_sh + u3
            a3 = a3 * a_sh
            d *= 2
        h = hcar[:, ls]
        hs = []
        for gi in range(groups):
            hg = u3[gi] + a3[gi] * h
            hs.append(hg)
            h = hg[SUBLANES - 1:SUBLANES, :]
        hcar[:, ls] = h
        rec = jnp.concatenate(hs, axis=0) * _gelu_tanh(y_ref[:, ls])
        rec_scr[:, ls] = rec
        ss = ss + jnp.sum(rec * rec, axis=-1, keepdims=True)
    xpad[0:SUBLANES, :] = x_ref[tt - SUBLANES:tt, :]
    o_ref[...] = (rec_scr[...] * lax.rsqrt(ss / c_all + EPS) * g_ref[...]).astype(o_ref.dtype)


def rec_branch(xy, conv_w, conv_b, w_a, b_a, w_x, b_x, lam, g_out, *, batch, tt, lane_chunk):
    t, c2 = xy.shape
    c = c2 // 2
    nt = t // batch // tt
    row = lambda v: v.reshape(1, c).astype(F32)
    vec = pl.BlockSpec((1, c), lambda b, i: (0, 0))
    wspec = pl.BlockSpec(w_a.shape, lambda b, i: (0, 0, 0))
    return pl.pallas_call(
        functools.partial(_rec_kernel, lane_chunk=lane_chunk),
        grid=(batch, nt),
        in_specs=[pl.BlockSpec((tt, c), lambda b, i: (b * nt + i, 0)),
                  pl.BlockSpec((tt, c), lambda b, i: (b * nt + i, 1)),
                  pl.BlockSpec((CONV_WIDTH, c), lambda b, i: (0, 0)),
                  vec, wspec, wspec, vec, vec, vec, vec],
        out_specs=pl.BlockSpec((tt, c), lambda b, i: (b * nt + i, 0)),
        out_shape=jax.ShapeDtypeStruct((t, c), BF16),
        scratch_shapes=[pltpu.VMEM((SUBLANES + tt, c), F32),
                        pltpu.VMEM((1, c), F32),
                        pltpu.VMEM((tt, c), F32)],
        compiler_params=_params("parallel", "arbitrary"),
        name="rec_branch",
    )(xy, xy, conv_w.astype(F32), row(conv_b), w_a.astype(BF16), w_x.astype(BF16),
      row(b_a), row(b_x), row(lam), row(g_out))


def _out_proj_kernel(a1_ref, a2_ref, b1_ref, b2_ref, x_ref, o_ref):
    acc = jnp.dot(a1_ref[...], b1_ref[...], preferred_element_type=F32)
    acc = acc + jnp.dot(a2_ref[...], b2_ref[...], preferred_element_type=F32)
    o_ref[...] = x_ref[...] + acc


def out_proj(a1, a2, w, x, *, tm, tn):
    t, k1 = a1.shape
    n = w.shape[1]
    return pl.pallas_call(
        _out_proj_kernel,
        grid=(t // tm, n // tn),
        in_specs=[pl.BlockSpec((tm, k1), lambda i, j: (i, 0)),
                  pl.BlockSpec((tm, k1), lambda i, j: (i, 0)),
                  pl.BlockSpec((k1, tn), lambda i, j: (0, j)),
                  pl.BlockSpec((k1, tn), lambda i, j: (1, j)),
                  pl.BlockSpec((tm, tn), lambda i, j: (i, j))],
        out_specs=pl.BlockSpec((tm, tn), lambda i, j: (i, j)),
        out_shape=jax.ShapeDtypeStruct((t, n), F32),
        compiler_params=_params("parallel", "arbitrary"),
        name="out_proj",
    )(a1, a2, w, w, x)


def _peer_scores_kernel(a_ref, b_ref, keys_ref, st_ref):
    q = jnp.dot(a_ref[...], b_ref[...], preferred_element_type=F32).astype(BF16)
    for c in range(keys_ref.shape[0]):
        rows = slice(c * PEER_N_KEYS, (c + 1) * PEER_N_KEYS)
        st_ref[rows, :] = lax.dot_general(keys_ref[c], q[:, rows], _NT,
                                          preferred_element_type=F32)


def peer_scores(h, w_q, keys, *, tm, tn):
    t, k = h.shape
    n = w_q.shape[1]
    kb = tn // PEER_N_KEYS
    return pl.pallas_call(
        _peer_scores_kernel,
        grid=(t // tm, n // tn),
        in_specs=[pl.BlockSpec((tm, k), lambda i, j: (i, 0)),
                  pl.BlockSpec((k, tn), lambda i, j: (0, j)),
                  pl.BlockSpec((kb,) + keys.shape[1:], lambda i, j: (j, 0, 0))],
        out_specs=pl.BlockSpec((tn, tm), lambda i, j: (j, i)),
        out_shape=jax.ShapeDtypeStruct((n, t), F32),
        compiler_params=_params("parallel", "arbitrary"),
        name="peer_scores",
    )(h, w_q, keys)


_CAND_PAIRS = [(a, b) for a in range(PEER_TOPK) for b in range(PEER_TOPK)
               if (a + 1) * (b + 1) <= PEER_TOPK]


def _top_values(s, count):
    rows = []
    for _ in range(count):
        m = jnp.max(s, axis=0, keepdims=True)
        rows.append(m)
        s = jnp.where(s == m, NEG, s)
    return rows


def _peer_topk_kernel(s_ref, tau_ref, e2_ref, s1t_ref, ct_ref, c_scr):
    nh, _, nk, _ = s_ref.shape

    def head(h, carry):
        s1 = s_ref[h, 0]
        s2 = s_ref[h, 1]
        t1 = _top_values(s1, PEER_TOPK)
        t2 = _top_values(s2, PEER_TOPK)
        cands = jnp.concatenate([t1[a] + t2[b] for a, b in _CAND_PAIRS], axis=0)
        tau = _top_values(cands, PEER_TOPK)[-1]
        z = jnp.sum(jnp.where(cands >= tau, jnp.exp(cands - (t1[0] + t2[0])), 0.0),
                    axis=0, keepdims=True)
        tau_ref[pl.ds(h, 1), :] = tau
        c_scr[h] = jnp.exp(s1 - t1[0]) / z
        e2_ref[h] = jnp.exp(s2 - t2[0])
        return carry

    lax.fori_loop(0, nh, head, 0)
    for h in range(nh):
        for lc in range(s1t_ref.shape[0]):
            ls = slice(lc * LANES, (lc + 1) * LANES)
            s1t_ref[lc, pl.ds(h, nk, stride=nh), :] = s_ref[h, 0, :, ls]
            ct_ref[lc, pl.ds(h, nk, stride=nh), :] = c_scr[h, :, ls]


def peer_topk(st, *, tt):
    nh, _, nk, t = st.shape
    return pl.pallas_call(
        _peer_topk_kernel,
        grid=(t // tt,),
        in_specs=[pl.BlockSpec((nh, 2, nk, tt), lambda i: (0, 0, 0, i))],
        out_specs=[pl.BlockSpec((nh, tt), lambda i: (0, i)),
                   pl.BlockSpec((nh, nk, tt), lambda i: (0, 0, i)),
                   pl.BlockSpec((tt // LANES, nk * nh, LANES), lambda i: (i, 0, 0)),
                   pl.BlockSpec((tt // LANES, nk * nh, LANES), lambda i: (i, 0, 0))],
        out_shape=[jax.ShapeDtypeStruct((nh, t), F32),
                   jax.ShapeDtypeStruct((nh, nk, t), F32),
                   jax.ShapeDtypeStruct((t // LANES, nk * nh, LANES), F32),
                   jax.ShapeDtypeStruct((t // LANES, nk * nh, LANES), F32)],
        scratch_shapes=[pltpu.VMEM((nh, nk, tt), F32)],
        compiler_params=_params("parallel"),
        name="peer_topk",
    )(st)


def _peer_dense_kernel(h_ref, u_ref, v_ref, s2_ref, e2_ref, tau_ref, s1t_ref, ct_ref, x_ref,
                       o_ref, act_scr, w_scr):
    te, tt = act_scr.shape
    nh, nk = s2_ref.shape[0], s2_ref.shape[1]
    def expert_acts():
        act_scr[...] = lax.dot_general(u_ref[...], h_ref[...], _NT, preferred_element_type=F32)

    def gate_block():
        for ii in range(te // nk):
            for lc in range(tt // LANES):
                ls = slice(lc * LANES, (lc + 1) * LANES)
                gate = jnp.zeros((nk, LANES), F32)
                for h in range(nh):
                    r = ii * nh + h
                    pair_sum = s2_ref[h, :, ls] + s1t_ref[lc, r:r + 1, :]
                    weight = e2_ref[h, :, ls] * ct_ref[lc, r:r + 1, :]
                    gate = gate + jnp.where(pair_sum >= tau_ref[h:h + 1, ls], weight, 0.0)
                w_tile = gate * _gelu_tanh(act_scr[ii * nk:(ii + 1) * nk, ls])
                w_scr[ls, ii * nk:(ii + 1) * nk] = w_tile.T.astype(w_scr.dtype)

    def accumulate_block():
        o_ref[...] += jnp.dot(w_scr[...], v_ref[...], preferred_element_type=F32)

    @pl.when(pl.program_id(1) == 0)
    def _():
        o_ref[...] = x_ref[...]

    expert_acts()
    gate_block()
    accumulate_block()


def peer_dense(h, u, v, st, tau, e2, s1t, ct, x, *, tt, te):
    t, d = h.shape
    e = u.shape[0]
    nh, _, nk, _ = st.shape
    once = pl.Buffered(1)
    key_rows = te // nk * nh
    return pl.pallas_call(
        _peer_dense_kernel,
        grid=(t // tt, e // te),
        in_specs=[pl.BlockSpec((tt, d), lambda i, j: (i, 0), pipeline_mode=once),
                  pl.BlockSpec((te, d), lambda i, j: (j, 0)),
                  pl.BlockSpec((te, d), lambda i, j: (j, 0)),
                  pl.BlockSpec((nh, None, nk, tt), lambda i, j: (0, 1, 0, i), pipeline_mode=once),
                  pl.BlockSpec((nh, nk, tt), lambda i, j: (0, 0, i), pipeline_mode=once),
                  pl.BlockSpec((nh, tt), lambda i, j: (0, i), pipeline_mode=once),
                  pl.BlockSpec((tt // LANES, key_rows, LANES), lambda i, j: (i, j, 0)),
                  pl.BlockSpec((tt // LANES, key_rows, LANES), lambda i, j: (i, j, 0)),
                  pl.BlockSpec((tt, d), lambda i, j: (i, 0), pipeline_mode=once)],
        out_specs=pl.BlockSpec((tt, d), lambda i, j: (i, 0)),
        out_shape=jax.ShapeDtypeStruct((t, d), F32),
        scratch_shapes=[pltpu.VMEM((te, tt), F32), pltpu.VMEM((tt, te), BF16)],
        compiler_params=_params("parallel", "arbitrary"),
        name="peer_dense",
    )(h, u, v, st, e2, tau, s1t, ct, x)


def hybrid_layer(x, g_norm1, w_in, g_q_norm, g_k_norm, rel_bias, conv_w, conv_b,
                 w_rg_a, b_rg_a, w_rg_x, b_rg_x, rg_lambda, g_attn_out, g_rec_out,
                 w_out, g_norm2, w_peer_q, peer_keys_1, peer_keys_2, peer_u, peer_v,
                 *, cfg):
    b, s, d = x.shape
    t = b * s
    d_attn = g_attn_out.shape[0]
    d_rec = g_rec_out.shape[0]
    n_heads = d_attn // HEAD_DIM
    x2 = x.reshape(t, d)

    h1 = rmsnorm_rows(x2, g_norm1, cfg["norm_tm"])
    w_in_b = w_in.astype(BF16)
    qkv = in_proj(h1, w_in_b, jnp.stack([g_q_norm, g_k_norm]), col0=0, n_cols=3 * d_attn,
                  n_norm_cols=2 * d_attn, out_dtype=BF16, tm=cfg["mm_tm"], tn=cfg["mm_tn"])
    xy = in_proj(h1, w_in_b, jnp.ones((1, HEAD_DIM), F32), col0=3 * d_attn, n_cols=2 * d_rec,
                 n_norm_cols=0, out_dtype=F32, tm=cfg["mm_tm"], tn=cfg["mm_tn"])

    attn = band_attention(qkv, rel_bias, seq=s, n_heads=n_heads, tq=cfg["attn_tq"],
                          heads_per_step=cfg["attn_heads"])
    attn_n = rmsnorm_rows(attn, g_attn_out, cfg["norm_tm"])
    rec_n = rec_branch(xy, conv_w, conv_b, w_rg_a, b_rg_a, w_rg_x, b_rg_x, rg_lambda, g_rec_out,
                       batch=b, tt=cfg["rec_tt"], lane_chunk=cfg["rec_lanes"])
    x1 = out_proj(attn_n, rec_n, w_out.astype(BF16), x2, tm=cfg["mm_tm"], tn=cfg["mm_tn"])

    h2 = rmsnorm_rows(x1, g_norm2, cfg["norm_tm"])
    nh, nk, half = peer_keys_1.shape
    keys = jnp.stack([peer_keys_1, peer_keys_2], axis=1).reshape(2 * nh, nk, half).astype(BF16)
    st = peer_scores(h2, w_peer_q.astype(BF16), keys, tm=cfg["mm_tm"], tn=cfg["mm_tn"])
    st = st.reshape(nh, 2, nk, t)
    tau, e2, s1t, ct = peer_topk(st, tt=cfg["topk_tt"])
    out = peer_dense(h2, peer_u.astype(BF16), peer_v.astype(BF16), st, tau, e2, s1t, ct, x1,
                     tt=cfg["peer_tt"], te=cfg["peer_te"])
    return out.reshape(b, s, d)


_CFG = dict(norm_tm=256, mm_tm=1024, mm_tn=1024, attn_tq=256, attn_heads=4, rec_tt=256, rec_lanes=512,
            topk_tt=256, peer_tt=512, peer_te=512)


def kernel(x, g_norm1, w_in, g_q_norm, g_k_norm, rel_bias, conv_w, conv_b, w_rg_a, b_rg_a,
           w_rg_x, b_rg_x, rg_lambda, g_attn_out, g_rec_out, w_out, g_norm2, w_peer_q,
           peer_keys_1, peer_keys_2, peer_u, peer_v):
    for l in range(g_norm1.shape[0]):
        x = hybrid_layer(x, g_norm1[l], w_in[l], g_q_norm[l], g_k_norm[l], rel_bias[l],
                         conv_w[l], conv_b[l], w_rg_a[l], b_rg_a[l], w_rg_x[l], b_rg_x[l],
                         rg_lambda[l], g_attn_out[l], g_rec_out[l], w_out[l], g_norm2[l],
                         w_peer_q[l], peer_keys_1[l], peer_keys_2[l], peer_u[l], peer_v[l],
                         cfg=_CFG)
    return x
```

```python
import functools
import math

import jax
import jax.numpy as jnp
from jax import lax
from jax.experimental import pallas as pl
from jax.experimental.pallas import tpu as pltpu

EPS = 1e-6
CHUNK = 64
LEFT_CHUNKS = 8
LEFT = LEFT_CHUNKS * CHUNK
MAX_REL = 256
HEAD_DIM = 128
REC_BLOCK_DIM = 128
CONV_WIDTH = 4
RG_C = 8.0
PEER_HEADS = 8
PEER_N_KEYS = 128
PEER_TOPK = 16
LANES = 128
SUBLANES = 8
NEG = -1e30
VMEM_LIMIT_BYTES = 56 * 1024 * 1024

F32 = jnp.float32
BF16 = jnp.bfloat16
_NT = (((1,), (1,)), ((), ()))


def _params(*sem):
    return pltpu.CompilerParams(dimension_semantics=sem, vmem_limit_bytes=VMEM_LIMIT_BYTES)


def _gelu_tanh(x):
    c = math.sqrt(2.0 / math.pi)
    return x * (0.5 * (1.0 + jnp.tanh(c * (x + 0.044715 * (x * x * x)))))


def _rmsnorm_kernel(x_ref, g_ref, o_ref):
    x = x_ref[...].astype(F32)
    ms = jnp.mean(x * x, axis=-1, keepdims=True)
    o_ref[...] = (x * lax.rsqrt(ms + EPS) * g_ref[...]).astype(o_ref.dtype)


def rmsnorm_rows(x, g, tm):
    t, d = x.shape
    return pl.pallas_call(
        _rmsnorm_kernel,
        grid=(t // tm,),
        in_specs=[pl.BlockSpec((tm, d), lambda i: (i, 0)),
                  pl.BlockSpec((1, d), lambda i: (0, 0))],
        out_specs=pl.BlockSpec((tm, d), lambda i: (i, 0)),
        out_shape=jax.ShapeDtypeStruct((t, d), BF16),
        compiler_params=_params("parallel"),
        name="rmsnorm_rows",
    )(x, g.reshape(1, d).astype(F32))


def _in_proj_kernel(a_ref, b_ref, g_ref, o_ref, *, n_norm_tiles):
    acc = jnp.dot(a_ref[...], b_ref[...], preferred_element_type=F32)
    j = pl.program_id(1)
    tn = o_ref.shape[1]

    @pl.when(j < n_norm_tiles)
    def _():
        g = g_ref[0]
        for c in range(tn // HEAD_DIM):
            cols = slice(c * HEAD_DIM, (c + 1) * HEAD_DIM)
            blk = acc[:, cols]
            ms = jnp.mean(blk * blk, axis=-1, keepdims=True)
            o_ref[:, cols] = (blk * lax.rsqrt(ms + EPS) * g).astype(o_ref.dtype)

    @pl.when(j >= n_norm_tiles)
    def _():
        o_ref[...] = acc.astype(o_ref.dtype)


def in_proj(h, w, gains, *, col0, n_cols, n_norm_cols, out_dtype, tm, tn):
    t, k = h.shape
    n_sec = gains.shape[0]
    sec_tiles = max(n_norm_cols // tn // n_sec, 1)
    jb0 = col0 // tn
    return pl.pallas_call(
        functools.partial(_in_proj_kernel, n_norm_tiles=n_norm_cols // tn),
        grid=(t // tm, n_cols // tn),
        in_specs=[pl.BlockSpec((tm, k), lambda i, j: (i, 0)),
                  pl.BlockSpec((k, tn), lambda i, j: (0, j + jb0)),
                  pl.BlockSpec((1, 1, HEAD_DIM),
                               lambda i, j: (jnp.minimum(j // sec_tiles, n_sec - 1), 0, 0))],
        out_specs=pl.BlockSpec((tm, tn), lambda i, j: (i, j)),
        out_shape=jax.ShapeDtypeStruct((t, n_cols), out_dtype),
        compiler_params=_params("parallel", "arbitrary"),
        name="in_proj",
    )(h, w, gains.reshape(n_sec, 1, HEAD_DIM).astype(F32))


def _rel_bias_diagonals(rel_bias, tq):
    w = LEFT + tq
    length = pl.cdiv(w + tq - 1, LANES) * LANES
    m = jnp.arange(length)
    d = jnp.where(m < w, m, m - length)
    rel = jnp.clip(LEFT - d, -MAX_REL, MAX_REL) + MAX_REL
    return rel_bias.astype(F32)[:, None, rel]


def _attn_kernel(*refs, n_prev, blocks_per_seq, scale):
    q_ref = refs[0]
    k_refs = refs[1:2 + n_prev]
    v_refs = refs[2 + n_prev:3 + 2 * n_prev]
    diag_ref, o_ref, tbl_scr = refs[3 + 2 * n_prev:]
    heads, tq, w = tbl_scr.shape
    i = pl.program_id(1)

    @pl.when(i == 0)
    def _():
        r = lax.broadcasted_iota(jnp.int32, (tq, w), 0)
        c = lax.broadcasted_iota(jnp.int32, (tq, w), 1)
        dchunk = r // CHUNK - c // CHUNK + LEFT_CHUNKS
        band = (dchunk >= 0) & (dchunk <= LEFT_CHUNKS)
        for hh in range(heads):
            diag = jnp.broadcast_to(diag_ref[hh], (tq, diag_ref.shape[2]))
            bias = pltpu.roll(diag, 0, axis=1, stride=1, stride_axis=0)[:, :w]
            tbl_scr[hh] = jnp.where(band, bias, NEG)

    jb = i % blocks_per_seq
    col = lax.broadcasted_iota(jnp.int32, (tq, w), 1)
    in_seq = col >= jnp.maximum(n_prev - jb, 0) * tq
    for hh in range(heads):
        cols = slice(hh * HEAD_DIM, (hh + 1) * HEAD_DIM)
        k = jnp.concatenate([r[:, cols] for r in k_refs], axis=0)
        v = jnp.concatenate([r[:, cols] for r in v_refs], axis=0)
        s = lax.dot_general(q_ref[:, cols], k, _NT, preferred_element_type=F32)
        s = jnp.where(in_seq, s * scale + tbl_scr[hh], NEG)
        m = jnp.max(s, axis=-1, keepdims=True)
        p = jnp.exp(s - m)
        l = jnp.sum(p, axis=-1, keepdims=True)
        o = jnp.dot(p.astype(v.dtype), v, preferred_element_type=F32)
        o_ref[:, cols] = (o / l).astype(o_ref.dtype)


def band_attention(qkv, rel_bias, *, seq, n_heads, tq, heads_per_step):
    t = qkv.shape[0]
    n_prev = LEFT // tq
    bps = seq // tq
    hp = heads_per_step
    groups = n_heads // hp
    diags = _rel_bias_diagonals(rel_bias, tq)

    def win_spec(col_base, back):
        def idx(g, i):
            first = (i // bps) * bps
            return (jnp.maximum(i - back, first), col_base + g)
        return pl.BlockSpec((tq, hp * HEAD_DIM), idx)

    backs = list(range(n_prev, -1, -1))
    in_specs = ([pl.BlockSpec((tq, hp * HEAD_DIM), lambda g, i: (i, g))]
                + [win_spec(groups, b) for b in backs]
                + [win_spec(2 * groups, b) for b in backs]
                + [pl.BlockSpec((hp, 1, diags.shape[2]), lambda g, i: (g, 0, 0))])
    n_win = n_prev + 1
    return pl.pallas_call(
        functools.partial(_attn_kernel, n_prev=n_prev, blocks_per_seq=bps,
                          scale=HEAD_DIM ** -0.5),
        grid=(groups, t // tq),
        in_specs=in_specs,
        out_specs=pl.BlockSpec((tq, hp * HEAD_DIM), lambda g, i: (i, g)),
        out_shape=jax.ShapeDtypeStruct((t, n_heads * HEAD_DIM), F32),
        scratch_shapes=[pltpu.VMEM((hp, tq, LEFT + tq), F32)],
        compiler_params=_params("parallel", "arbitrary"),
        name="band_attention",
    )(*([qkv] * (1 + 2 * n_win)), diags)


def _rec_kernel(x_ref, y_ref, cw_ref, cb_ref, wa_ref, wx_ref, ba_ref, bx_ref, lam_ref, g_ref,
                o_ref, xpad, hcar, rec_scr, *, lane_chunk):
    tt, c_all = x_ref.shape
    groups = tt // SUBLANES
    pad0 = SUBLANES - (CONV_WIDTH - 1)

    @pl.when(pl.program_id(1) == 0)
    def _():
        xpad[0:SUBLANES, :] = jnp.zeros((SUBLANES, c_all), F32)
        hcar[...] = jnp.zeros_like(hcar)

    xpad[SUBLANES:SUBLANES + tt, :] = x_ref[...]
    ss = jnp.zeros((tt, 1), F32)
    row = lax.broadcasted_iota(jnp.int32, (groups, SUBLANES, lane_chunk), 1)
    for cbk in range(c_all // lane_chunk):
        ls = slice(cbk * lane_chunk, (cbk + 1) * lane_chunk)
        xr = cb_ref[:, ls]
        for tap in range(CONV_WIDTH):
            xr = xr + xpad[pad0 + tap:pad0 + tap + tt, ls] * cw_ref[tap:tap + 1, ls]
        xr_b = xr.astype(BF16)
        pre_a, pre_x = [], []
        for n in range(lane_chunk // REC_BLOCK_DIM):
            nb = cbk * (lane_chunk // REC_BLOCK_DIM) + n
            xs = xr_b[:, n * REC_BLOCK_DIM:(n + 1) * REC_BLOCK_DIM]
            pre_a.append(jnp.dot(xs, wa_ref[nb], preferred_element_type=F32))
            pre_x.append(jnp.dot(xs, wx_ref[nb], preferred_element_type=F32))
        r = jax.nn.sigmoid(jnp.concatenate(pre_a, axis=1) + ba_ref[:, ls])
        gate_i = jax.nn.sigmoid(jnp.concatenate(pre_x, axis=1) + bx_ref[:, ls])
        z = -lam_ref[:, ls]
        softplus = jnp.maximum(z, 0.0) + jnp.log1p(jnp.exp(-jnp.abs(z)))
        log_a = (-RG_C * r) * softplus
        a = jnp.exp(log_a)
        mult = jnp.sqrt(-jnp.tanh(log_a) * (a * a + 1.0))
        u = mult * (gate_i * xr)
        a3 = a.reshape(groups, SUBLANES, lane_chunk)
        u3 = u.reshape(groups, SUBLANES, lane_chunk)
        d = 1
        while d < SUBLANES:
            keep = row >= d
            a_sh = jnp.where(keep, pltpu.roll(a3, d, axis=1), 1.0)
            u_sh = jnp.where(keep, pltpu.roll(u3, d, axis=1), 0.0)
            u3 = a3 * u_sh + u3
            a3 = a3 * a_sh
            d *= 2
        h = hcar[:, ls]
        hs = []
        for gi in range(groups):
            hg = u3[gi] + a3[gi] * h
            hs.append(hg)
            h = hg[SUBLANES - 1:SUBLANES, :]
        hcar[:, ls] = h
        rec = jnp.concatenate(hs, axis=0) * _gelu_tanh(y_ref[:, ls])
        rec_scr[:, ls] = rec
        ss = ss + jnp.sum(rec * rec, axis=-1, keepdims=True)
    xpad[0:SUBLANES, :] = x_ref[tt - SUBLANES:tt, :]
    o_ref[...] = (rec_scr[...] * lax.rsqrt(ss / c_all + EPS) * g_ref[...]).astype(o_ref.dtype)


def rec_branch(xy, conv_w, conv_b, w_a, b_a, w_x, b_x, lam, g_out, *, batch, tt, lane_chunk):
    t, c2 = xy.shape
    c = c2 // 2
    nt = t // batch // tt
    row = lambda v: v.reshape(1, c).astype(F32)
    vec = pl.BlockSpec((1, c), lambda b, i: (0, 0))
    wspec = pl.BlockSpec(w_a.shape, lambda b, i: (0, 0, 0))
    return pl.pallas_call(
        functools.partial(_rec_kernel, lane_chunk=lane_chunk),
        grid=(batch, nt),
        in_specs=[pl.BlockSpec((tt, c), lambda b, i: (b * nt + i, 0)),
                  pl.BlockSpec((tt, c), lambda b, i: (b * nt + i, 1)),
                  pl.BlockSpec((CONV_WIDTH, c), lambda b, i: (0, 0)),
                  vec, wspec, wspec, vec, vec, vec, vec],
        out_specs=pl.BlockSpec((tt, c), lambda b, i: (b * nt + i, 0)),
        out_shape=jax.ShapeDtypeStruct((t, c), BF16),
        scratch_shapes=[pltpu.VMEM((SUBLANES + tt, c), F32),
                        pltpu.VMEM((1, c), F32),
                        pltpu.VMEM((tt, c), F32)],
        compiler_params=_params("parallel", "arbitrary"),
        name="rec_branch",
    )(xy, xy, conv_w.astype(F32), row(conv_b), w_a.astype(BF16), w_x.astype(BF16),
      row(b_a), row(b_x), row(lam), row(g_out))


def _out_proj_kernel(a1_ref, a2_ref, b1_ref, b2_ref, x_ref, o_ref):
    acc = jnp.dot(a1_ref[...], b1_ref[...], preferred_element_type=F32)
    acc = acc + jnp.dot(a2_ref[...], b2_ref[...], preferred_element_type=F32)
    o_ref[...] = x_ref[...] + acc


def out_proj(a1, a2, w, x, *, tm, tn):
    t, k1 = a1.shape
    n = w.shape[1]
    return pl.pallas_call(
        _out_proj_kernel,
        grid=(t // tm, n // tn),
        in_specs=[pl.BlockSpec((tm, k1), lambda i, j: (i, 0)),
                  pl.BlockSpec((tm, k1), lambda i, j: (i, 0)),
                  pl.BlockSpec((k1, tn), lambda i, j: (0, j)),
                  pl.BlockSpec((k1, tn), lambda i, j: (1, j)),
                  pl.BlockSpec((tm, tn), lambda i, j: (i, j))],
        out_specs=pl.BlockSpec((tm, tn), lambda i, j: (i, j)),
        out_shape=jax.ShapeDtypeStruct((t, n), F32),
        compiler_params=_params("parallel", "arbitrary"),
        name="out_proj",
    )(a1, a2, w, w, x)


def _peer_scores_kernel(a_ref, b_ref, keys_ref, st_ref):
    q = jnp.dot(a_ref[...], b_ref[...], preferred_element_type=F32).astype(BF16)
    for c in range(keys_ref.shape[0]):
        rows = slice(c * PEER_N_KEYS, (c + 1) * PEER_N_KEYS)
        st_ref[rows, :] = lax.dot_general(keys_ref[c], q[:, rows], _NT,
                                          preferred_element_type=F32)


def peer_scores(h, w_q, keys, *, tm, tn):
    t, k = h.shape
    n = w_q.shape[1]
    kb = tn // PEER_N_KEYS
    return pl.pallas_call(
        _peer_scores_kernel,
        grid=(t // tm, n // tn),
        in_specs=[pl.BlockSpec((tm, k), lambda i, j: (i, 0)),
                  pl.BlockSpec((k, tn), lambda i, j: (0, j)),
                  pl.BlockSpec((kb,) + keys.shape[1:], lambda i, j: (j, 0, 0))],
        out_specs=pl.BlockSpec((tn, tm), lambda i, j: (j, i)),
        out_shape=jax.ShapeDtypeStruct((n, t), F32),
        compiler_params=_params("parallel", "arbitrary"),
        name="peer_scores",
    )(h, w_q, keys)


_CAND_PAIRS = [(a, b) for a in range(PEER_TOPK + 1) for b in range(PEER_TOPK + 1)
               if (a + 1) * (b + 1) <= PEER_TOPK + 1]


def _top_values(s, count):
    rows = []
    for _ in range(count):
        m = jnp.max(s, axis=0, keepdims=True)
        rows.append(m)
        s = jnp.where(s == m, NEG, s)
    return rows


def _peer_topk_kernel(s_ref, e2_ref, thr_ref, hc_ref, thr_scr, hc_scr):
    nh, _, nk, _ = s_ref.shape

    def head(h, carry):
        s1 = s_ref[h, 0]
        s2 = s_ref[h, 1]
        t1 = _top_values(s1, PEER_TOPK + 1)
        t2 = _top_values(s2, PEER_TOPK + 1)
        cands = jnp.concatenate([t1[a] + t2[b] for a, b in _CAND_PAIRS], axis=0)
        top = _top_values(cands, PEER_TOPK + 1)
        tau = 0.5 * (top[PEER_TOPK - 1] + top[PEER_TOPK])
        z = jnp.sum(jnp.where(cands > tau, jnp.exp(cands - top[0]), 0.0), axis=0, keepdims=True)
        thr_scr[h] = tau - s1
        hc_scr[h] = (0.5 * jnp.exp(s1 - t1[0])) / z
        e2_ref[h] = jnp.exp(s2 - t2[0])
        return carry

    lax.fori_loop(0, nh, head, 0)
    for h in range(nh):
        for lc in range(thr_ref.shape[0]):
            ls = slice(lc * LANES, (lc + 1) * LANES)
            thr_ref[lc, pl.ds(h, nk, stride=nh), :] = thr_scr[h, :, ls]
            hc_ref[lc, pl.ds(h, nk, stride=nh), :] = hc_scr[h, :, ls]


def peer_topk(st, *, tt):
    nh, _, nk, t = st.shape
    key_major = pl.BlockSpec((tt // LANES, nk * nh, LANES), lambda i: (i, 0, 0))
    key_major_shape = jax.ShapeDtypeStruct((t // LANES, nk * nh, LANES), F32)
    return pl.pallas_call(
        _peer_topk_kernel,
        grid=(t // tt,),
        in_specs=[pl.BlockSpec((nh, 2, nk, tt), lambda i: (0, 0, 0, i))],
        out_specs=[pl.BlockSpec((nh, nk, tt), lambda i: (0, 0, i)), key_major, key_major],
        out_shape=[jax.ShapeDtypeStruct((nh, nk, t), F32), key_major_shape, key_major_shape],
        scratch_shapes=[pltpu.VMEM((nh, nk, tt), F32), pltpu.VMEM((nh, nk, tt), F32)],
        compiler_params=_params("parallel"),
        name="peer_topk",
    )(st)


_GELU_C0 = math.sqrt(2.0 / math.pi)
_GELU_C1 = 0.044715 * _GELU_C0


def _peer_dense_kernel(h_ref, u_ref, v_ref, s2_ref, e2_ref, thr_ref, hc_ref, x_ref,
                       o_ref, act_scr, w_scr):
    te, tt = act_scr.shape
    nh, nk = s2_ref.shape[0], s2_ref.shape[1]

    @pl.when(pl.program_id(1) == 0)
    def _():
        o_ref[...] = x_ref[...]

    act_scr[...] = lax.dot_general(u_ref[...], h_ref[...], _NT, preferred_element_type=F32)
    for ii in range(te // nk):
        for lc in range(tt // LANES):
            ls = slice(lc * LANES, (lc + 1) * LANES)
            half_gate = None
            for h in range(nh):
                r = ii * nh + h
                weight = e2_ref[h, :, ls] * hc_ref[lc, r:r + 1, :]
                part = jnp.where(s2_ref[h, :, ls] >= thr_ref[lc, r:r + 1, :], weight, 0.0)
                half_gate = part if half_gate is None else half_gate + part
            a = act_scr[ii * nk:(ii + 1) * nk, ls]
            t = jnp.tanh(a * (_GELU_C0 + _GELU_C1 * (a * a)))
            w_tile = (half_gate * a) * (1.0 + t)
            w_scr[ls, ii * nk:(ii + 1) * nk] = w_tile.T.astype(w_scr.dtype)
    o_ref[...] += jnp.dot(w_scr[...], v_ref[...], preferred_element_type=F32)


def peer_dense(h, u, v, st, e2, thr, half_c, x, *, tt, te):
    t, d = h.shape
    e = u.shape[0]
    nh, _, nk, _ = st.shape
    once = pl.Buffered(1)
    key_rows = te // nk * nh
    return pl.pallas_call(
        _peer_dense_kernel,
        grid=(t // tt, e // te),
        in_specs=[pl.BlockSpec((tt, d), lambda i, j: (i, 0), pipeline_mode=once),
                  pl.BlockSpec((te, d), lambda i, j: (j, 0)),
                  pl.BlockSpec((te, d), lambda i, j: (j, 0)),
                  pl.BlockSpec((nh, None, nk, tt), lambda i, j: (0, 1, 0, i), pipeline_mode=once),
                  pl.BlockSpec((nh, nk, tt), lambda i, j: (0, 0, i), pipeline_mode=once),
                  pl.BlockSpec((tt // LANES, key_rows, LANES), lambda i, j: (i, j, 0)),
                  pl.BlockSpec((tt // LANES, key_rows, LANES), lambda i, j: (i, j, 0)),
                  pl.BlockSpec((tt, d), lambda i, j: (i, 0), pipeline_mode=once)],
        out_specs=pl.BlockSpec((tt, d), lambda i, j: (i, 0)),
        out_shape=jax.ShapeDtypeStruct((t, d), F32),
        scratch_shapes=[pltpu.VMEM((te, tt), F32), pltpu.VMEM((tt, te), BF16)],
        compiler_params=_params("parallel", "arbitrary"),
        name="peer_dense",
    )(h, u, v, st, e2, thr, half_c, x)


def hybrid_layer(x, g_norm1, w_in, g_q_norm, g_k_norm, rel_bias, conv_w, conv_b,
                 w_rg_a, b_rg_a, w_rg_x, b_rg_x, rg_lambda, g_attn_out, g_rec_out,
                 w_out, g_norm2, w_peer_q, peer_keys_1, peer_keys_2, peer_u, peer_v,
                 *, cfg):
    b, s, d = x.shape
    t = b * s
    d_attn = g_attn_out.shape[0]
    d_rec = g_rec_out.shape[0]
    n_heads = d_attn // HEAD_DIM
    x2 = x.reshape(t, d)

    h1 = rmsnorm_rows(x2, g_norm1, cfg["norm_tm"])
    w_in_b = w_in.astype(BF16)
    qkv = in_proj(h1, w_in_b, jnp.stack([g_q_norm, g_k_norm]), col0=0, n_cols=3 * d_attn,
                  n_norm_cols=2 * d_attn, out_dtype=BF16, tm=cfg["mm_tm"], tn=cfg["mm_tn"])
    xy = in_proj(h1, w_in_b, jnp.ones((1, HEAD_DIM), F32), col0=3 * d_attn, n_cols=2 * d_rec,
                 n_norm_cols=0, out_dtype=F32, tm=cfg["mm_tm"], tn=cfg["mm_tn"])

    attn = band_attention(qkv, rel_bias, seq=s, n_heads=n_heads, tq=cfg["attn_tq"],
                          heads_per_step=cfg["attn_heads"])
    attn_n = rmsnorm_rows(attn, g_attn_out, cfg["norm_tm"])
    rec_n = rec_branch(xy, conv_w, conv_b, w_rg_a, b_rg_a, w_rg_x, b_rg_x, rg_lambda, g_rec_out,
                       batch=b, tt=cfg["rec_tt"], lane_chunk=cfg["rec_lanes"])
    x1 = out_proj(attn_n, rec_n, w_out.astype(BF16), x2, tm=cfg["mm_tm"], tn=cfg["mm_tn"])

    h2 = rmsnorm_rows(x1, g_norm2, cfg["norm_tm"])
    nh, nk, half = peer_keys_1.shape
    keys = jnp.stack([peer_keys_1, peer_keys_2], axis=1).reshape(2 * nh, nk, half).astype(BF16)
    st = peer_scores(h2, w_peer_q.astype(BF16), keys, tm=cfg["mm_tm"], tn=cfg["mm_tn"])
    st = st.reshape(nh, 2, nk, t)
    e2, thr, half_c = peer_topk(st, tt=cfg["topk_tt"])
    out = peer_dense(h2, peer_u.astype(BF16), peer_v.astype(BF16), st, e2, thr, half_c, x1,
                     tt=cfg["peer_tt"], te=cfg["peer_te"])
    return out.reshape(b, s, d)


_CFG = dict(norm_tm=256, mm_tm=1024, mm_tn=1024, attn_tq=256, attn_heads=4, rec_tt=256, rec_lanes=512,
            topk_tt=256, peer_tt=512, peer_te=512)


def kernel(x, g_norm1, w_in, g_q_norm, g_k_norm, rel_bias, conv_w, conv_b, w_rg_a, b_rg_a,
           w_rg_x, b_rg_x, rg_lambda, g_attn_out, g_rec_out, w_out, g_norm2, w_peer_q,
           peer_keys_1, peer_keys_2, peer_u, peer_v):
    for l in range(g_norm1.shape[0]):
        x = hybrid_layer(x, g_norm1[l], w_in[l], g_q_norm[l], g_k_norm[l], rel_bias[l],
                         conv_w[l], conv_b[l], w_rg_a[l], b_rg_a[l], w_rg_x[l], b_rg_x[l],
                         rg_lambda[l], g_attn_out[l], g_rec_out[l], w_out[l], g_norm2[l],
                         w_peer_q[l], peer_keys_1[l], peer_keys_2[l], peer_u[l], peer_v[l],
                         cfg=_CFG)
    return x
```

```python
import functools
import math

import jax
import jax.numpy as jnp
from jax import lax
from jax.experimental import pallas as pl
from jax.experimental.pallas import tpu as pltpu

EPS = 1e-6
CHUNK = 64
LEFT_CHUNKS = 8
LEFT = LEFT_CHUNKS * CHUNK
MAX_REL = 256
HEAD_DIM = 128
REC_BLOCK_DIM = 128
CONV_WIDTH = 4
RG_C = 8.0
PEER_HEADS = 8
PEER_N_KEYS = 128
PEER_TOPK = 16
LANES = 128
SUBLANES = 8
NEG = -1e30
VMEM_LIMIT_BYTES = 56 * 1024 * 1024

F32 = jnp.float32
BF16 = jnp.bfloat16
_NT = (((1,), (1,)), ((), ()))


def _params(*sem):
    return pltpu.CompilerParams(dimension_semantics=sem, vmem_limit_bytes=VMEM_LIMIT_BYTES)


_GELU_C0 = math.sqrt(2.0 / math.pi)
_GELU_C1 = 0.044715 * _GELU_C0


def _gelu_tanh(x):
    return (0.5 * x) * (1.0 + jnp.tanh(x * (_GELU_C0 + _GELU_C1 * (x * x))))


def _sigmoid(x):
    return 0.5 + 0.5 * jnp.tanh(0.5 * x)


def _rmsnorm_kernel(x_ref, g_ref, o_ref):
    x = x_ref[...].astype(F32)
    ms = jnp.mean(x * x, axis=-1, keepdims=True)
    o_ref[...] = (x * lax.rsqrt(ms + EPS) * g_ref[...]).astype(o_ref.dtype)


def rmsnorm_rows(x, g, tm):
    t, d = x.shape
    return pl.pallas_call(
        _rmsnorm_kernel,
        grid=(t // tm,),
        in_specs=[pl.BlockSpec((tm, d), lambda i: (i, 0)),
                  pl.BlockSpec((1, d), lambda i: (0, 0))],
        out_specs=pl.BlockSpec((tm, d), lambda i: (i, 0)),
        out_shape=jax.ShapeDtypeStruct((t, d), BF16),
        compiler_params=_params("parallel"),
        name="rmsnorm_rows",
    )(x, g.reshape(1, d).astype(F32))


def _in_proj_kernel(a_ref, b_ref, g_ref, o_ref, *, n_norm_tiles):
    acc = jnp.dot(a_ref[...], b_ref[...], preferred_element_type=F32)
    j = pl.program_id(1)
    tn = o_ref.shape[1]

    @pl.when(j < n_norm_tiles)
    def _():
        g = g_ref[0]
        for c in range(tn // HEAD_DIM):
            cols = slice(c * HEAD_DIM, (c + 1) * HEAD_DIM)
            blk = acc[:, cols]
            ms = jnp.mean(blk * blk, axis=-1, keepdims=True)
            o_ref[:, cols] = (blk * lax.rsqrt(ms + EPS) * g).astype(o_ref.dtype)

    @pl.when(j >= n_norm_tiles)
    def _():
        o_ref[...] = acc.astype(o_ref.dtype)


def in_proj(h, w, gains, *, col0, n_cols, n_norm_cols, out_dtype, tm, tn):
    t, k = h.shape
    n_sec = gains.shape[0]
    sec_tiles = max(n_norm_cols // tn // n_sec, 1)
    jb0 = col0 // tn
    return pl.pallas_call(
        functools.partial(_in_proj_kernel, n_norm_tiles=n_norm_cols // tn),
        grid=(t // tm, n_cols // tn),
        in_specs=[pl.BlockSpec((tm, k), lambda i, j: (i, 0)),
                  pl.BlockSpec((k, tn), lambda i, j: (0, j + jb0)),
                  pl.BlockSpec((1, 1, HEAD_DIM),
                               lambda i, j: (jnp.minimum(j // sec_tiles, n_sec - 1), 0, 0))],
        out_specs=pl.BlockSpec((tm, tn), lambda i, j: (i, j)),
        out_shape=jax.ShapeDtypeStruct((t, n_cols), out_dtype),
        compiler_params=_params("parallel", "arbitrary"),
        name="in_proj",
    )(h, w, gains.reshape(n_sec, 1, HEAD_DIM).astype(F32))


def _rel_bias_diagonals(rel_bias, tq):
    w = LEFT + tq
    length = pl.cdiv(w + tq - 1, LANES) * LANES
    m = jnp.arange(length)
    d = jnp.where(m < w, m, m - length)
    rel = jnp.clip(LEFT - d, -MAX_REL, MAX_REL) + MAX_REL
    return rel_bias.astype(F32)[:, None, rel]


def _attn_kernel(*refs, n_prev, blocks_per_seq, scale):
    q_ref = refs[0]
    k_refs = refs[1:2 + n_prev]
    v_refs = refs[2 + n_prev:3 + 2 * n_prev]
    diag_ref, g_ref, o_ref, tbl_scr, o_scr = refs[3 + 2 * n_prev:]
    heads, tq, w = tbl_scr.shape
    i = pl.program_id(0)

    @pl.when(i == 0)
    def _():
        r = lax.broadcasted_iota(jnp.int32, (tq, w), 0)
        c = lax.broadcasted_iota(jnp.int32, (tq, w), 1)
        dchunk = r // CHUNK - c // CHUNK + LEFT_CHUNKS
        band = (dchunk >= 0) & (dchunk <= LEFT_CHUNKS)
        for hh in range(heads):
            diag = jnp.broadcast_to(diag_ref[hh], (tq, diag_ref.shape[2]))
            bias = pltpu.roll(diag, 0, axis=1, stride=1, stride_axis=0)[:, :w]
            tbl_scr[hh] = jnp.where(band, bias, NEG)

    jb = i % blocks_per_seq
    col = lax.broadcasted_iota(jnp.int32, (tq, w), 1)
    in_seq = col >= jnp.maximum(n_prev - jb, 0) * tq
    for hh in range(heads):
        cols = slice(hh * HEAD_DIM, (hh + 1) * HEAD_DIM)
        k = jnp.concatenate([r[:, cols] for r in k_refs], axis=0)
        v = jnp.concatenate([r[:, cols] for r in v_refs], axis=0)
        s = lax.dot_general(q_ref[:, cols], k, _NT, preferred_element_type=F32)
        s = jnp.where(in_seq, s * scale + tbl_scr[hh], NEG)
        m = jnp.max(s, axis=-1, keepdims=True)
        p = jnp.exp(s - m)
        l = jnp.sum(p, axis=-1, keepdims=True)
        o = jnp.dot(p.astype(v.dtype), v, preferred_element_type=F32)
        o_scr[:, cols] = o / l
    attn = o_scr[...]
    ms = jnp.mean(attn * attn, axis=-1, keepdims=True)
    o_ref[...] = (attn * lax.rsqrt(ms + EPS) * g_ref[...]).astype(o_ref.dtype)


def band_attention(qkv, rel_bias, g_out, *, seq, n_heads, tq):
    t = qkv.shape[0]
    n_prev = LEFT // tq
    bps = seq // tq
    d_attn = n_heads * HEAD_DIM
    diags = _rel_bias_diagonals(rel_bias, tq)

    def win_spec(col_block, back):
        def idx(i):
            first = (i // bps) * bps
            return (jnp.maximum(i - back, first), col_block)
        return pl.BlockSpec((tq, d_attn), idx)

    backs = list(range(n_prev, -1, -1))
    in_specs = ([pl.BlockSpec((tq, d_attn), lambda i: (i, 0))]
                + [win_spec(1, b) for b in backs]
                + [win_spec(2, b) for b in backs]
                + [pl.BlockSpec(diags.shape, lambda i: (0, 0, 0)),
                   pl.BlockSpec((1, d_attn), lambda i: (0, 0))])
    n_win = n_prev + 1
    return pl.pallas_call(
        functools.partial(_attn_kernel, n_prev=n_prev, blocks_per_seq=bps,
                          scale=HEAD_DIM ** -0.5),
        grid=(t // tq,),
        in_specs=in_specs,
        out_specs=pl.BlockSpec((tq, d_attn), lambda i: (i, 0)),
        out_shape=jax.ShapeDtypeStruct((t, d_attn), BF16),
        scratch_shapes=[pltpu.VMEM((n_heads, tq, LEFT + tq), F32),
                        pltpu.VMEM((tq, d_attn), F32)],
        compiler_params=_params("arbitrary"),
        name="band_attention",
    )(*([qkv] * (1 + 2 * n_win)), diags, g_out.reshape(1, d_attn).astype(F32))


def _rec_kernel(x_ref, y_ref, cw_ref, cb_ref, wa_ref, wx_ref, ba_ref, bx_ref, lam_ref, g_ref,
                o_ref, xpad, hcar, rec_scr, *, lane_chunk):
    tt, c_all = x_ref.shape
    groups = tt // SUBLANES
    pad0 = SUBLANES - (CONV_WIDTH - 1)

    @pl.when(pl.program_id(1) == 0)
    def _():
        xpad[0:SUBLANES, :] = jnp.zeros((SUBLANES, c_all), F32)
        hcar[...] = jnp.zeros_like(hcar)

    xpad[SUBLANES:SUBLANES + tt, :] = x_ref[...]
    ss = jnp.zeros((tt, 1), F32)
    row = lax.broadcasted_iota(jnp.int32, (groups, SUBLANES, lane_chunk), 1)
    for cbk in range(c_all // lane_chunk):
        ls = slice(cbk * lane_chunk, (cbk + 1) * lane_chunk)
        xr = cb_ref[:, ls]
        for tap in range(CONV_WIDTH):
            xr = xr + xpad[pad0 + tap:pad0 + tap + tt, ls] * cw_ref[tap:tap + 1, ls]
        xr_b = xr.astype(BF16)
        pre_a, pre_x = [], []
        for n in range(lane_chunk // REC_BLOCK_DIM):
            nb = cbk * (lane_chunk // REC_BLOCK_DIM) + n
            xs = xr_b[:, n * REC_BLOCK_DIM:(n + 1) * REC_BLOCK_DIM]
            pre_a.append(jnp.dot(xs, wa_ref[nb], preferred_element_type=F32))
            pre_x.append(jnp.dot(xs, wx_ref[nb], preferred_element_type=F32))
        r = _sigmoid(jnp.concatenate(pre_a, axis=1) + ba_ref[:, ls])
        gate_i = _sigmoid(jnp.concatenate(pre_x, axis=1) + bx_ref[:, ls])
        z = -lam_ref[:, ls]
        softplus = jnp.maximum(z, 0.0) + jnp.log1p(jnp.exp(-jnp.abs(z)))
        log_a = (-RG_C * r) * softplus
        a = jnp.exp(log_a)
        m2 = -jnp.tanh(log_a) * (a * a + 1.0)
        mult = jnp.where(m2 > 0.0, m2 * lax.rsqrt(m2), 0.0)
        u = mult * (gate_i * xr)
        a3 = a.reshape(groups, SUBLANES, lane_chunk)
        u3 = u.reshape(groups, SUBLANES, lane_chunk)
        d = 1
        while d < SUBLANES:
            keep = row >= d
            a_sh = jnp.where(keep, pltpu.roll(a3, d, axis=1), 1.0)
            u_sh = jnp.where(keep, pltpu.roll(u3, d, axis=1), 0.0)
            u3 = a3 * u_sh + u3
            a3 = a3 * a_sh
            d *= 2
        h = hcar[:, ls]
        hs = []
        for gi in range(groups):
            hg = u3[gi] + a3[gi] * h
            hs.append(hg)
            h = hg[SUBLANES - 1:SUBLANES, :]
        hcar[:, ls] = h
        rec = jnp.concatenate(hs, axis=0) * _gelu_tanh(y_ref[:, ls])
        rec_scr[:, ls] = rec
        ss = ss + jnp.sum(rec * rec, axis=-1, keepdims=True)
    xpad[0:SUBLANES, :] = x_ref[tt - SUBLANES:tt, :]
    o_ref[...] = (rec_scr[...] * lax.rsqrt(ss / c_all + EPS) * g_ref[...]).astype(o_ref.dtype)


def rec_branch(xy, conv_w, conv_b, w_a, b_a, w_x, b_x, lam, g_out, *, batch, tt, lane_chunk):
    t, c2 = xy.shape
    c = c2 // 2
    nt = t // batch // tt
    row = lambda v: v.reshape(1, c).astype(F32)
    vec = pl.BlockSpec((1, c), lambda b, i: (0, 0))
    wspec = pl.BlockSpec(w_a.shape, lambda b, i: (0, 0, 0))
    return pl.pallas_call(
        functools.partial(_rec_kernel, lane_chunk=lane_chunk),
        grid=(batch, nt),
        in_specs=[pl.BlockSpec((tt, c), lambda b, i: (b * nt + i, 0)),
                  pl.BlockSpec((tt, c), lambda b, i: (b * nt + i, 1)),
                  pl.BlockSpec((CONV_WIDTH, c), lambda b, i: (0, 0)),
                  vec, wspec, wspec, vec, vec, vec, vec],
        out_specs=pl.BlockSpec((tt, c), lambda b, i: (b * nt + i, 0)),
        out_shape=jax.ShapeDtypeStruct((t, c), BF16),
        scratch_shapes=[pltpu.VMEM((SUBLANES + tt, c), F32),
                        pltpu.VMEM((1, c), F32),
                        pltpu.VMEM((tt, c), F32)],
        compiler_params=_params("parallel", "arbitrary"),
        name="rec_branch",
    )(xy, xy, conv_w.astype(F32), row(conv_b), w_a.astype(BF16), w_x.astype(BF16),
      row(b_a), row(b_x), row(lam), row(g_out))


def _out_proj_kernel(a1_ref, a2_ref, b1_ref, b2_ref, x_ref, o_ref):
    acc = jnp.dot(a1_ref[...], b1_ref[...], preferred_element_type=F32)
    acc = acc + jnp.dot(a2_ref[...], b2_ref[...], preferred_element_type=F32)
    o_ref[...] = x_ref[...] + acc


def out_proj(a1, a2, w, x, *, tm, tn):
    t, k1 = a1.shape
    n = w.shape[1]
    return pl.pallas_call(
        _out_proj_kernel,
        grid=(t // tm, n // tn),
        in_specs=[pl.BlockSpec((tm, k1), lambda i, j: (i, 0)),
                  pl.BlockSpec((tm, k1), lambda i, j: (i, 0)),
                  pl.BlockSpec((k1, tn), lambda i, j: (0, j)),
                  pl.BlockSpec((k1, tn), lambda i, j: (1, j)),
                  pl.BlockSpec((tm, tn), lambda i, j: (i, j))],
        out_specs=pl.BlockSpec((tm, tn), lambda i, j: (i, j)),
        out_shape=jax.ShapeDtypeStruct((t, n), F32),
        compiler_params=_params("parallel", "arbitrary"),
        name="out_proj",
    )(a1, a2, w, w, x)


def _peer_scores_kernel(a_ref, b_ref, keys_ref, st_ref):
    q = jnp.dot(a_ref[...], b_ref[...], preferred_element_type=F32).astype(BF16)
    for c in range(keys_ref.shape[0]):
        rows = slice(c * PEER_N_KEYS, (c + 1) * PEER_N_KEYS)
        st_ref[rows, :] = lax.dot_general(keys_ref[c], q[:, rows], _NT,
                                          preferred_element_type=F32)


def peer_scores(h, w_q, keys, *, tm, tn):
    t, k = h.shape
    n = w_q.shape[1]
    kb = tn // PEER_N_KEYS
    return pl.pallas_call(
        _peer_scores_kernel,
        grid=(t // tm, n // tn),
        in_specs=[pl.BlockSpec((tm, k), lambda i, j: (i, 0)),
                  pl.BlockSpec((k, tn), lambda i, j: (0, j)),
                  pl.BlockSpec((kb,) + keys.shape[1:], lambda i, j: (j, 0, 0))],
        out_specs=pl.BlockSpec((tn, tm), lambda i, j: (j, i)),
        out_shape=jax.ShapeDtypeStruct((n, t), F32),
        compiler_params=_params("parallel", "arbitrary"),
        name="peer_scores",
    )(h, w_q, keys)


_CAND_PAIRS = [(a, b) for a in range(PEER_TOPK + 1) for b in range(PEER_TOPK + 1)
               if (a + 1) * (b + 1) <= PEER_TOPK + 1]


def _top_values(s, count):
    rows = []
    for _ in range(count):
        m = jnp.max(s, axis=0, keepdims=True)
        rows.append(m)
        s = jnp.where(s == m, NEG, s)
    return rows


def _peer_topk_kernel(s_ref, e2_ref, thr_ref, hc_ref, thr_scr, hc_scr):
    nh, _, nk, _ = s_ref.shape

    def head(h, carry):
        s1 = s_ref[h, 0]
        s2 = s_ref[h, 1]
        t1 = _top_values(s1, PEER_TOPK + 1)
        t2 = _top_values(s2, PEER_TOPK + 1)
        cands = jnp.concatenate([t1[a] + t2[b] for a, b in _CAND_PAIRS], axis=0)
        top = _top_values(cands, PEER_TOPK + 1)
        tau = 0.5 * (top[PEER_TOPK - 1] + top[PEER_TOPK])
        z = jnp.sum(jnp.where(cands > tau, jnp.exp(cands - top[0]), 0.0), axis=0, keepdims=True)
        thr_scr[h] = tau - s1
        hc_scr[h] = (0.5 * jnp.exp(s1 - t1[0])) / z
        e2_ref[h] = jnp.exp(s2 - t2[0])
        return carry

    lax.fori_loop(0, nh, head, 0)
    for h in range(nh):
        for lc in range(thr_ref.shape[0]):
            ls = slice(lc * LANES, (lc + 1) * LANES)
            thr_ref[lc, pl.ds(h, nk, stride=nh), :] = thr_scr[h, :, ls]
            hc_ref[lc, pl.ds(h, nk, stride=nh), :] = hc_scr[h, :, ls]


def peer_topk(st, *, tt):
    nh, _, nk, t = st.shape
    key_major = pl.BlockSpec((tt // LANES, nk * nh, LANES), lambda i: (i, 0, 0))
    key_major_shape = jax.ShapeDtypeStruct((t // LANES, nk * nh, LANES), F32)
    return pl.pallas_call(
        _peer_topk_kernel,
        grid=(t // tt,),
        in_specs=[pl.BlockSpec((nh, 2, nk, tt), lambda i: (0, 0, 0, i))],
        out_specs=[pl.BlockSpec((nh, nk, tt), lambda i: (0, 0, i)), key_major, key_major],
        out_shape=[jax.ShapeDtypeStruct((nh, nk, t), F32), key_major_shape, key_major_shape],
        scratch_shapes=[pltpu.VMEM((nh, nk, tt), F32), pltpu.VMEM((nh, nk, tt), F32)],
        compiler_params=_params("parallel"),
        name="peer_topk",
    )(st)


def _peer_dense_kernel(h_ref, u_ref, v_ref, s2_ref, e2_ref, thr_ref, hc_ref, x_ref,
                       o_ref, act_scr, w_scr):
    te, tt = act_scr.shape
    nh, nk = s2_ref.shape[0], s2_ref.shape[1]

    @pl.when(pl.program_id(1) == 0)
    def _():
        o_ref[...] = x_ref[...]

    act_scr[...] = lax.dot_general(u_ref[...], h_ref[...], _NT, preferred_element_type=F32)
    for ii in range(te // nk):
        for lc in range(tt // LANES):
            ls = slice(lc * LANES, (lc + 1) * LANES)
            half_gate = None
            for h in range(nh):
                r = ii * nh + h
                weight = e2_ref[h, :, ls] * hc_ref[lc, r:r + 1, :]
                part = jnp.where(s2_ref[h, :, ls] >= thr_ref[lc, r:r + 1, :], weight, 0.0)
                half_gate = part if half_gate is None else half_gate + part
            a = act_scr[ii * nk:(ii + 1) * nk, ls]
            t = jnp.tanh(a * (_GELU_C0 + _GELU_C1 * (a * a)))
            w_tile = (half_gate * a) * (1.0 + t)
            w_scr[ls, ii * nk:(ii + 1) * nk] = w_tile.T.astype(w_scr.dtype)
    o_ref[...] += jnp.dot(w_scr[...], v_ref[...], preferred_element_type=F32)


def peer_dense(h, u, v, st, e2, thr, half_c, x, *, tt, te):
    t, d = h.shape
    e = u.shape[0]
    nh, _, nk, _ = st.shape
    once = pl.Buffered(1)
    key_rows = te // nk * nh
    return pl.pallas_call(
        _peer_dense_kernel,
        grid=(t // tt, e // te),
        in_specs=[pl.BlockSpec((tt, d), lambda i, j: (i, 0), pipeline_mode=once),
                  pl.BlockSpec((te, d), lambda i, j: (j, 0)),
                  pl.BlockSpec((te, d), lambda i, j: (j, 0)),
                  pl.BlockSpec((nh, None, nk, tt), lambda i, j: (0, 1, 0, i), pipeline_mode=once),
                  pl.BlockSpec((nh, nk, tt), lambda i, j: (0, 0, i), pipeline_mode=once),
                  pl.BlockSpec((tt // LANES, key_rows, LANES), lambda i, j: (i, j, 0)),
                  pl.BlockSpec((tt // LANES, key_rows, LANES), lambda i, j: (i, j, 0)),
                  pl.BlockSpec((tt, d), lambda i, j: (i, 0), pipeline_mode=once)],
        out_specs=pl.BlockSpec((tt, d), lambda i, j: (i, 0)),
        out_shape=jax.ShapeDtypeStruct((t, d), F32),
        scratch_shapes=[pltpu.VMEM((te, tt), F32), pltpu.VMEM((tt, te), BF16)],
        compiler_params=_params("parallel", "arbitrary"),
        name="peer_dense",
    )(h, u, v, st, e2, thr, half_c, x)


def hybrid_layer(x, g_norm1, w_in, g_q_norm, g_k_norm, rel_bias, conv_w, conv_b,
                 w_rg_a, b_rg_a, w_rg_x, b_rg_x, rg_lambda, g_attn_out, g_rec_out,
                 w_out, g_norm2, w_peer_q, peer_keys_1, peer_keys_2, peer_u, peer_v,
                 *, cfg):
    b, s, d = x.shape
    t = b * s
    d_attn = g_attn_out.shape[0]
    d_rec = g_rec_out.shape[0]
    n_heads = d_attn // HEAD_DIM
    x2 = x.reshape(t, d)

    h1 = rmsnorm_rows(x2, g_norm1, cfg["norm_tm"])
    w_in_b = w_in.astype(BF16)
    qkv = in_proj(h1, w_in_b, jnp.stack([g_q_norm, g_k_norm]), col0=0, n_cols=3 * d_attn,
                  n_norm_cols=2 * d_attn, out_dtype=BF16, tm=cfg["mm_tm"], tn=cfg["mm_tn"])
    xy = in_proj(h1, w_in_b, jnp.ones((1, HEAD_DIM), F32), col0=3 * d_attn, n_cols=2 * d_rec,
                 n_norm_cols=0, out_dtype=F32, tm=cfg["mm_tm"], tn=cfg["mm_tn"])

    attn_n = band_attention(qkv, rel_bias, g_attn_out, seq=s, n_heads=n_heads,
                            tq=cfg["attn_tq"])
    rec_n = rec_branch(xy, conv_w, conv_b, w_rg_a, b_rg_a, w_rg_x, b_rg_x, rg_lambda, g_rec_out,
                       batch=b, tt=cfg["rec_tt"], lane_chunk=cfg["rec_lanes"])
    x1 = out_proj(attn_n, rec_n, w_out.astype(BF16), x2, tm=cfg["mm_tm"], tn=cfg["mm_tn"])

    h2 = rmsnorm_rows(x1, g_norm2, cfg["norm_tm"])
    nh, nk, half = peer_keys_1.shape
    keys = jnp.stack([peer_keys_1, peer_keys_2], axis=1).reshape(2 * nh, nk, half).astype(BF16)
    st = peer_scores(h2, w_peer_q.astype(BF16), keys, tm=cfg["mm_tm"], tn=cfg["mm_tn"])
    st = st.reshape(nh, 2, nk, t)
    e2, thr, half_c = peer_topk(st, tt=cfg["topk_tt"])
    out = peer_dense(h2, peer_u.astype(BF16), peer_v.astype(BF16), st, e2, thr, half_c, x1,
                     tt=cfg["peer_tt"], te=cfg["peer_te"])
    return out.reshape(b, s, d)


_CFG = dict(norm_tm=256, mm_tm=1024, mm_tn=1024, attn_tq=256, rec_tt=256, rec_lanes=512,
            topk_tt=256, peer_tt=512, peer_te=512)


def kernel(x, g_norm1, w_in, g_q_norm, g_k_norm, rel_bias, conv_w, conv_b, w_rg_a, b_rg_a,
           w_rg_x, b_rg_x, rg_lambda, g_attn_out, g_rec_out, w_out, g_norm2, w_peer_q,
           peer_keys_1, peer_keys_2, peer_u, peer_v):
    for l in range(g_norm1.shape[0]):
        x = hybrid_layer(x, g_norm1[l], w_in[l], g_q_norm[l], g_k_norm[l], rel_bias[l],
                         conv_w[l], conv_b[l], w_rg_a[l], b_rg_a[l], w_rg_x[l], b_rg_x[l],
                         rg_lambda[l], g_attn_out[l], g_rec_out[l], w_out[l], g_norm2[l],
                         w_peer_q[l], peer_keys_1[l], peer_keys_2[l], peer_u[l], peer_v[l],
                         cfg=_CFG)
    return x
```

```python
import functools
import math

import jax
import jax.numpy as jnp
from jax import lax
from jax.experimental import pallas as pl
from jax.experimental.pallas import tpu as pltpu

EPS = 1e-6
CHUNK = 64
LEFT_CHUNKS = 8
LEFT = LEFT_CHUNKS * CHUNK
MAX_REL = 256
HEAD_DIM = 128
REC_BLOCK_DIM = 128
CONV_WIDTH = 4
RG_C = 8.0
PEER_HEADS = 8
PEER_N_KEYS = 128
PEER_TOPK = 16
LANES = 128
SUBLANES = 8
NEG = -1e30
LOG2_E = math.log2(math.e)
VMEM_LIMIT_BYTES = 60000 * 1024

F32 = jnp.float32
BF16 = jnp.bfloat16
_NT = (((1,), (1,)), ((), ()))


def _params(*sem):
    return pltpu.CompilerParams(dimension_semantics=sem, vmem_limit_bytes=VMEM_LIMIT_BYTES)


_GELU_C0 = math.sqrt(2.0 / math.pi)
_GELU_C1 = 0.044715 * _GELU_C0


def _gelu_tanh(x):
    return (0.5 * x) * (1.0 + jnp.tanh(x * (_GELU_C0 + _GELU_C1 * (x * x))))


def _sigmoid(x):
    return 0.5 + 0.5 * jnp.tanh(0.5 * x)


def _rmsnorm_kernel(x_ref, g_ref, o_ref):
    x = x_ref[...].astype(F32)
    ms = jnp.mean(x * x, axis=-1, keepdims=True)
    o_ref[...] = (x * lax.rsqrt(ms + EPS) * g_ref[...]).astype(o_ref.dtype)


def rmsnorm_rows(x, g, tm):
    t, d = x.shape
    return pl.pallas_call(
        _rmsnorm_kernel,
        grid=(t // tm,),
        in_specs=[pl.BlockSpec((tm, d), lambda i: (i, 0)),
                  pl.BlockSpec((1, d), lambda i: (0, 0))],
        out_specs=pl.BlockSpec((tm, d), lambda i: (i, 0)),
        out_shape=jax.ShapeDtypeStruct((t, d), BF16),
        compiler_params=_params("parallel"),
        name="rmsnorm_rows",
    )(x, g.reshape(1, d).astype(F32))


def _in_proj_kernel(a_ref, b_ref, g_ref, o_ref, *, n_norm_tiles):
    acc = jnp.dot(a_ref[...], b_ref[...], preferred_element_type=F32)
    j = pl.program_id(1)
    tn = o_ref.shape[1]

    @pl.when(j < n_norm_tiles)
    def _():
        g = g_ref[0]
        for c in range(tn // HEAD_DIM):
            cols = slice(c * HEAD_DIM, (c + 1) * HEAD_DIM)
            blk = acc[:, cols]
            ms = jnp.mean(blk * blk, axis=-1, keepdims=True)
            o_ref[:, cols] = (blk * lax.rsqrt(ms + EPS) * g).astype(o_ref.dtype)

    @pl.when(j >= n_norm_tiles)
    def _():
        o_ref[...] = acc.astype(o_ref.dtype)


def in_proj(h, w, gains, *, col0, n_cols, n_norm_cols, out_dtype, tm, tn):
    t, k = h.shape
    n_sec = gains.shape[0]
    sec_tiles = max(n_norm_cols // tn // n_sec, 1)
    jb0 = col0 // tn
    return pl.pallas_call(
        functools.partial(_in_proj_kernel, n_norm_tiles=n_norm_cols // tn),
        grid=(t // tm, n_cols // tn),
        in_specs=[pl.BlockSpec((tm, k), lambda i, j: (i, 0)),
                  pl.BlockSpec((k, tn), lambda i, j: (0, j + jb0)),
                  pl.BlockSpec((1, 1, HEAD_DIM),
                               lambda i, j: (jnp.minimum(j // sec_tiles, n_sec - 1), 0, 0))],
        out_specs=pl.BlockSpec((tm, tn), lambda i, j: (i, j)),
        out_shape=jax.ShapeDtypeStruct((t, n_cols), out_dtype),
        compiler_params=_params("parallel", "arbitrary"),
        name="in_proj",
    )(h, w, gains.reshape(n_sec, 1, HEAD_DIM).astype(F32))


def _rel_bias_diagonals(rel_bias, tq):
    w = LEFT + tq
    length = pl.cdiv(w + tq - 1, LANES) * LANES
    m = jnp.arange(length)
    d = jnp.where(m < w, m, m - length)
    rel = jnp.clip(LEFT - d, -MAX_REL, MAX_REL) + MAX_REL
    return rel_bias.astype(F32)[:, None, rel]


def _attn_kernel(*refs, n_prev, blocks_per_seq, scale):
    q_ref = refs[0]
    k_refs = refs[1:2 + n_prev]
    v_refs = refs[2 + n_prev:3 + 2 * n_prev]
    diag_ref, g_ref, o_ref, tbl_scr, o_scr = refs[3 + 2 * n_prev:]
    heads, tq, w = tbl_scr.shape
    i = pl.program_id(0)

    @pl.when(i == 0)
    def _():
        r = lax.broadcasted_iota(jnp.int32, (tq, w), 0)
        c = lax.broadcasted_iota(jnp.int32, (tq, w), 1)
        dchunk = r // CHUNK - c // CHUNK + LEFT_CHUNKS
        band = (dchunk >= 0) & (dchunk <= LEFT_CHUNKS)
        for hh in range(heads):
            diag = jnp.broadcast_to(diag_ref[hh], (tq, diag_ref.shape[2]))
            bias = pltpu.roll(diag, 0, axis=1, stride=1, stride_axis=0)[:, :w]
            tbl_scr[hh] = jnp.where(band, bias * LOG2_E, NEG)

    def heads_loop(mask_fn):
        for hh in range(heads):
            cols = slice(hh * HEAD_DIM, (hh + 1) * HEAD_DIM)
            k = jnp.concatenate([r[:, cols] for r in k_refs], axis=0)
            v = jnp.concatenate([r[:, cols] for r in v_refs], axis=0)
            s = lax.dot_general(q_ref[:, cols], k, _NT, preferred_element_type=F32)
            s = mask_fn(s * (scale * LOG2_E) + tbl_scr[hh])
            m = jnp.max(s, axis=-1, keepdims=True)
            p = jnp.exp2(s - m)
            l = jnp.sum(p, axis=-1, keepdims=True)
            o = jnp.dot(p.astype(v.dtype), v, preferred_element_type=F32)
            o_scr[:, cols] = o / l

    jb = i % blocks_per_seq

    @pl.when(jb < n_prev)
    def _():
        col = lax.broadcasted_iota(jnp.int32, (tq, w), 1)
        in_seq = col >= (n_prev - jb) * tq
        heads_loop(lambda s: jnp.where(in_seq, s, NEG))

    @pl.when(jb >= n_prev)
    def _():
        heads_loop(lambda s: s)

    attn = o_scr[...]
    ms = jnp.mean(attn * attn, axis=-1, keepdims=True)
    o_ref[...] = (attn * lax.rsqrt(ms + EPS) * g_ref[...]).astype(o_ref.dtype)


def band_attention(qkv, rel_bias, g_out, *, seq, n_heads, tq):
    t = qkv.shape[0]
    n_prev = LEFT // tq
    bps = seq // tq
    d_attn = n_heads * HEAD_DIM
    diags = _rel_bias_diagonals(rel_bias, tq)

    def win_spec(col_block, back):
        def idx(i):
            first = (i // bps) * bps
            return (jnp.maximum(i - back, first), col_block)
        return pl.BlockSpec((tq, d_attn), idx)

    backs = list(range(n_prev, -1, -1))
    in_specs = ([pl.BlockSpec((tq, d_attn), lambda i: (i, 0))]
                + [win_spec(1, b) for b in backs]
                + [win_spec(2, b) for b in backs]
                + [pl.BlockSpec(diags.shape, lambda i: (0, 0, 0)),
                   pl.BlockSpec((1, d_attn), lambda i: (0, 0))])
    n_win = n_prev + 1
    return pl.pallas_call(
        functools.partial(_attn_kernel, n_prev=n_prev, blocks_per_seq=bps,
                          scale=HEAD_DIM ** -0.5),
        grid=(t // tq,),
        in_specs=in_specs,
        out_specs=pl.BlockSpec((tq, d_attn), lambda i: (i, 0)),
        out_shape=jax.ShapeDtypeStruct((t, d_attn), BF16),
        scratch_shapes=[pltpu.VMEM((n_heads, tq, LEFT + tq), F32),
                        pltpu.VMEM((tq, d_attn), F32)],
        compiler_params=_params("arbitrary"),
        name="band_attention",
    )(*([qkv] * (1 + 2 * n_win)), diags, g_out.reshape(1, d_attn).astype(F32))


def _rec_kernel(x_ref, y_ref, cw_ref, cb_ref, wa_ref, wx_ref, ba_ref, bx_ref, lam_ref, g_ref,
                o_ref, xpad, hcar, rec_scr, *, lane_chunk):
    tt, c_all = x_ref.shape
    groups = tt // SUBLANES
    pad0 = SUBLANES - (CONV_WIDTH - 1)

    @pl.when(pl.program_id(1) == 0)
    def _():
        xpad[0:SUBLANES, :] = jnp.zeros((SUBLANES, c_all), F32)
        hcar[...] = jnp.zeros_like(hcar)

    xpad[SUBLANES:SUBLANES + tt, :] = x_ref[...]
    ss = jnp.zeros((tt, 1), F32)
    row = lax.broadcasted_iota(jnp.int32, (groups, SUBLANES, lane_chunk), 1)
    for cbk in range(c_all // lane_chunk):
        ls = slice(cbk * lane_chunk, (cbk + 1) * lane_chunk)
        xr = cb_ref[:, ls]
        for tap in range(CONV_WIDTH):
            xr = xr + xpad[pad0 + tap:pad0 + tap + tt, ls] * cw_ref[tap:tap + 1, ls]
        xr_b = xr.astype(BF16)
        pre_a, pre_x = [], []
        for n in range(lane_chunk // REC_BLOCK_DIM):
            nb = cbk * (lane_chunk // REC_BLOCK_DIM) + n
            xs = xr_b[:, n * REC_BLOCK_DIM:(n + 1) * REC_BLOCK_DIM]
            pre_a.append(jnp.dot(xs, wa_ref[nb], preferred_element_type=F32))
            pre_x.append(jnp.dot(xs, wx_ref[nb], preferred_element_type=F32))
        r = _sigmoid(jnp.concatenate(pre_a, axis=1) + ba_ref[:, ls])
        gate_i = _sigmoid(jnp.concatenate(pre_x, axis=1) + bx_ref[:, ls])
        z = -lam_ref[:, ls]
        softplus = jnp.maximum(z, 0.0) + jnp.log1p(jnp.exp(-jnp.abs(z)))
        log_a = (-RG_C * r) * softplus
        a = jnp.exp(log_a)
        m2 = -jnp.tanh(log_a) * (a * a + 1.0)
        mult = jnp.where(m2 > 0.0, m2 * lax.rsqrt(m2), 0.0)
        u = mult * (gate_i * xr)
        a3 = a.reshape(groups, SUBLANES, lane_chunk)
        u3 = u.reshape(groups, SUBLANES, lane_chunk)
        d = 1
        while d < SUBLANES:
            keep = row >= d
            a_sh = jnp.where(keep, pltpu.roll(a3, d, axis=1), 1.0)
            u_sh = jnp.where(keep, pltpu.roll(u3, d, axis=1), 0.0)
            u3 = a3 * u_sh + u3
            a3 = a3 * a_sh
            d *= 2
        h = hcar[:, ls]
        hs = []
        for gi in range(groups):
            hg = u3[gi] + a3[gi] * h
            hs.append(hg)
            h = hg[SUBLANES - 1:SUBLANES, :]
        hcar[:, ls] = h
        rec = jnp.concatenate(hs, axis=0) * _gelu_tanh(y_ref[:, ls])
        rec_scr[:, ls] = rec
        ss = ss + jnp.sum(rec * rec, axis=-1, keepdims=True)
    xpad[0:SUBLANES, :] = x_ref[tt - SUBLANES:tt, :]
    o_ref[...] = (rec_scr[...] * lax.rsqrt(ss / c_all + EPS) * g_ref[...]).astype(o_ref.dtype)


def rec_branch(xy, conv_w, conv_b, w_a, b_a, w_x, b_x, lam, g_out, *, batch, tt, lane_chunk):
    t, c2 = xy.shape
    c = c2 // 2
    nt = t // batch // tt
    row = lambda v: v.reshape(1, c).astype(F32)
    vec = pl.BlockSpec((1, c), lambda b, i: (0, 0))
    wspec = pl.BlockSpec(w_a.shape, lambda b, i: (0, 0, 0))
    return pl.pallas_call(
        functools.partial(_rec_kernel, lane_chunk=lane_chunk),
        grid=(batch, nt),
        in_specs=[pl.BlockSpec((tt, c), lambda b, i: (b * nt + i, 0)),
                  pl.BlockSpec((tt, c), lambda b, i: (b * nt + i, 1)),
                  pl.BlockSpec((CONV_WIDTH, c), lambda b, i: (0, 0)),
                  vec, wspec, wspec, vec, vec, vec, vec],
        out_specs=pl.BlockSpec((tt, c), lambda b, i: (b * nt + i, 0)),
        out_shape=jax.ShapeDtypeStruct((t, c), BF16),
        scratch_shapes=[pltpu.VMEM((SUBLANES + tt, c), F32),
                        pltpu.VMEM((1, c), F32),
                        pltpu.VMEM((tt, c), F32)],
        compiler_params=_params("parallel", "arbitrary"),
        name="rec_branch",
    )(xy, xy, conv_w.astype(F32), row(conv_b), w_a.astype(BF16), w_x.astype(BF16),
      row(b_a), row(b_x), row(lam), row(g_out))


def _out_proj_kernel(a1_ref, a2_ref, b1_ref, b2_ref, x_ref, o_ref):
    acc = jnp.dot(a1_ref[...], b1_ref[...], preferred_element_type=F32)
    acc = acc + jnp.dot(a2_ref[...], b2_ref[...], preferred_element_type=F32)
    o_ref[...] = x_ref[...] + acc


def out_proj(a1, a2, w, x, *, tm, tn):
    t, k1 = a1.shape
    n = w.shape[1]
    return pl.pallas_call(
        _out_proj_kernel,
        grid=(t // tm, n // tn),
        in_specs=[pl.BlockSpec((tm, k1), lambda i, j: (i, 0)),
                  pl.BlockSpec((tm, k1), lambda i, j: (i, 0)),
                  pl.BlockSpec((k1, tn), lambda i, j: (0, j)),
                  pl.BlockSpec((k1, tn), lambda i, j: (1, j)),
                  pl.BlockSpec((tm, tn), lambda i, j: (i, j))],
        out_specs=pl.BlockSpec((tm, tn), lambda i, j: (i, j)),
        out_shape=jax.ShapeDtypeStruct((t, n), F32),
        compiler_params=_params("parallel", "arbitrary"),
        name="out_proj",
    )(a1, a2, w, w, x)


def _peer_scores_kernel(a_ref, b_ref, keys_ref, st_ref):
    q = jnp.dot(a_ref[...], b_ref[...], preferred_element_type=F32).astype(BF16)
    for c in range(keys_ref.shape[0]):
        rows = slice(c * PEER_N_KEYS, (c + 1) * PEER_N_KEYS)
        st_ref[rows, :] = lax.dot_general(keys_ref[c], q[:, rows], _NT,
                                          preferred_element_type=F32)


def peer_scores(h, w_q, keys, *, tm, tn):
    t, k = h.shape
    n = w_q.shape[1]
    kb = tn // PEER_N_KEYS
    return pl.pallas_call(
        _peer_scores_kernel,
        grid=(t // tm, n // tn),
        in_specs=[pl.BlockSpec((tm, k), lambda i, j: (i, 0)),
                  pl.BlockSpec((k, tn), lambda i, j: (0, j)),
                  pl.BlockSpec((kb,) + keys.shape[1:], lambda i, j: (j, 0, 0))],
        out_specs=pl.BlockSpec((tn, tm), lambda i, j: (j, i)),
        out_shape=jax.ShapeDtypeStruct((n, t), F32),
        compiler_params=_params("parallel", "arbitrary"),
        name="peer_scores",
    )(h, w_q, keys)


_CAND_PAIRS = [(a, b) for a in range(PEER_TOPK + 1) for b in range(PEER_TOPK + 1)
               if (a + 1) * (b + 1) <= PEER_TOPK + 1]


def _top_values(s, count):
    rows = []
    for _ in range(count):
        m = jnp.max(s, axis=0, keepdims=True)
        rows.append(m)
        s = jnp.where(s == m, NEG, s)
    return rows


def _peer_topk_kernel(s_ref, e2_ref, thr_ref, hc_ref, thr_scr, hc_scr):
    nh, _, nk, _ = s_ref.shape

    def head(h, carry):
        s1 = s_ref[h, 0]
        s2 = s_ref[h, 1]
        t1 = _top_values(s1, PEER_TOPK + 1)
        t2 = _top_values(s2, PEER_TOPK + 1)
        cands = jnp.concatenate([t1[a] + t2[b] for a, b in _CAND_PAIRS], axis=0)
        top = _top_values(cands, PEER_TOPK + 1)
        tau = 0.5 * (top[PEER_TOPK - 1] + top[PEER_TOPK])
        z = jnp.sum(jnp.where(cands > tau, jnp.exp(cands - top[0]), 0.0), axis=0, keepdims=True)
        thr_scr[h] = tau - s1
        hc_scr[h] = (0.5 * jnp.exp(s1 - t1[0])) / z
        e2_ref[h] = jnp.exp(s2 - t2[0])
        return carry

    lax.fori_loop(0, nh, head, 0)
    for h in range(nh):
        for lc in range(thr_ref.shape[0]):
            ls = slice(lc * LANES, (lc + 1) * LANES)
            thr_ref[lc, pl.ds(h, nk, stride=nh), :] = thr_scr[h, :, ls]
            hc_ref[lc, pl.ds(h, nk, stride=nh), :] = hc_scr[h, :, ls]


def peer_topk(st, *, tt):
    nh, _, nk, t = st.shape
    key_major = pl.BlockSpec((tt // LANES, nk * nh, LANES), lambda i: (i, 0, 0))
    key_major_shape = jax.ShapeDtypeStruct((t // LANES, nk * nh, LANES), F32)
    return pl.pallas_call(
        _peer_topk_kernel,
        grid=(t // tt,),
        in_specs=[pl.BlockSpec((nh, 2, nk, tt), lambda i: (0, 0, 0, i))],
        out_specs=[pl.BlockSpec((nh, nk, tt), lambda i: (0, 0, i)), key_major, key_major],
        out_shape=[jax.ShapeDtypeStruct((nh, nk, t), F32), key_major_shape, key_major_shape],
        scratch_shapes=[pltpu.VMEM((nh, nk, tt), F32), pltpu.VMEM((nh, nk, tt), F32)],
        compiler_params=_params("parallel"),
        name="peer_topk",
    )(st)


def _peer_dense_kernel(h_ref, u_ref, v_ref, s2_ref, e2_ref, thr_ref, hc_ref, x_ref,
                       o_ref, act_scr, w_scr):
    te, tt = act_scr.shape
    nh, nk = s2_ref.shape[0], s2_ref.shape[1]

    @pl.when(pl.program_id(1) == 0)
    def _():
        o_ref[...] = x_ref[...]

    act_scr[...] = lax.dot_general(u_ref[...], h_ref[...], _NT, preferred_element_type=F32)
    for ii in range(te // nk):
        for lc in range(tt // LANES):
            ls = slice(lc * LANES, (lc + 1) * LANES)
            half_gate = None
            for h in range(nh):
                r = ii * nh + h
                weight = e2_ref[h, :, ls] * hc_ref[lc, r:r + 1, :]
                part = jnp.where(s2_ref[h, :, ls] >= thr_ref[lc, r:r + 1, :], weight, 0.0)
                half_gate = part if half_gate is None else half_gate + part
            a = act_scr[ii * nk:(ii + 1) * nk, ls]
            t = jnp.tanh(a * (_GELU_C0 + _GELU_C1 * (a * a)))
            w_tile = (half_gate * a) * (1.0 + t)
            w_scr[ls, ii * nk:(ii + 1) * nk] = w_tile.T.astype(w_scr.dtype)
    o_ref[...] += jnp.dot(w_scr[...], v_ref[...], preferred_element_type=F32)


def peer_dense(h, u, v, st, e2, thr, half_c, x, *, tt, te):
    t, d = h.shape
    e = u.shape[0]
    nh, _, nk, _ = st.shape
    once = pl.Buffered(1)
    key_rows = te // nk * nh
    return pl.pallas_call(
        _peer_dense_kernel,
        grid=(t // tt, e // te),
        in_specs=[pl.BlockSpec((tt, d), lambda i, j: (i, 0), pipeline_mode=once),
                  pl.BlockSpec((te, d), lambda i, j: (j, 0)),
                  pl.BlockSpec((te, d), lambda i, j: (j, 0)),
                  pl.BlockSpec((nh, None, nk, tt), lambda i, j: (0, 1, 0, i), pipeline_mode=once),
                  pl.BlockSpec((nh, nk, tt), lambda i, j: (0, 0, i), pipeline_mode=once),
                  pl.BlockSpec((tt // LANES, key_rows, LANES), lambda i, j: (i, j, 0)),
                  pl.BlockSpec((tt // LANES, key_rows, LANES), lambda i, j: (i, j, 0)),
                  pl.BlockSpec((tt, d), lambda i, j: (i, 0))],
        out_specs=pl.BlockSpec((tt, d), lambda i, j: (i, 0)),
        out_shape=jax.ShapeDtypeStruct((t, d), F32),
        scratch_shapes=[pltpu.VMEM((te, tt), F32), pltpu.VMEM((tt, te), BF16)],
        compiler_params=_params("parallel", "arbitrary"),
        name="peer_dense",
    )(h, u, v, st, e2, thr, half_c, x)


def hybrid_layer(x, g_norm1, w_in, g_q_norm, g_k_norm, rel_bias, conv_w, conv_b,
                 w_rg_a, b_rg_a, w_rg_x, b_rg_x, rg_lambda, g_attn_out, g_rec_out,
                 w_out, g_norm2, w_peer_q, peer_keys_1, peer_keys_2, peer_u, peer_v,
                 *, cfg):
    b, s, d = x.shape
    t = b * s
    d_attn = g_attn_out.shape[0]
    d_rec = g_rec_out.shape[0]
    n_heads = d_attn // HEAD_DIM
    x2 = x.reshape(t, d)

    h1 = rmsnorm_rows(x2, g_norm1, cfg["norm_tm"])
    w_in_b = w_in.astype(BF16)
    qkv = in_proj(h1, w_in_b, jnp.stack([g_q_norm, g_k_norm]), col0=0, n_cols=3 * d_attn,
                  n_norm_cols=2 * d_attn, out_dtype=BF16, tm=cfg["mm_tm"], tn=cfg["mm_tn"])
    xy = in_proj(h1, w_in_b, jnp.ones((1, HEAD_DIM), F32), col0=3 * d_attn, n_cols=2 * d_rec,
                 n_norm_cols=0, out_dtype=F32, tm=cfg["mm_tm"], tn=cfg["mm_tn"])

    attn_n = band_attention(qkv, rel_bias, g_attn_out, seq=s, n_heads=n_heads,
                            tq=cfg["attn_tq"])
    rec_n = rec_branch(xy, conv_w, conv_b, w_rg_a, b_rg_a, w_rg_x, b_rg_x, rg_lambda, g_rec_out,
                       batch=b, tt=cfg["rec_tt"], lane_chunk=cfg["rec_lanes"])
    x1 = out_proj(attn_n, rec_n, w_out.astype(BF16), x2, tm=cfg["mm_tm"], tn=cfg["mm_tn"])

    h2 = rmsnorm_rows(x1, g_norm2, cfg["norm_tm"])
    nh, nk, half = peer_keys_1.shape
    keys = jnp.stack([peer_keys_1, peer_keys_2], axis=1).reshape(2 * nh, nk, half).astype(BF16)
    st = peer_scores(h2, w_peer_q.astype(BF16), keys, tm=cfg["mm_tm"], tn=cfg["mm_tn"])
    st = st.reshape(nh, 2, nk, t)
    e2, thr, half_c = peer_topk(st, tt=cfg["topk_tt"])
    out = peer_dense(h2, peer_u.astype(BF16), peer_v.astype(BF16), st, e2, thr, half_c, x1,
                     tt=cfg["peer_tt"], te=cfg["peer_te"])
    return out.reshape(b, s, d)


_CFG = dict(norm_tm=256, mm_tm=1024, mm_tn=1024, attn_tq=256, rec_tt=256, rec_lanes=512,
            topk_tt=256, peer_tt=512, peer_te=512)


def kernel(x, g_norm1, w_in, g_q_norm, g_k_norm, rel_bias, conv_w, conv_b, w_rg_a, b_rg_a,
           w_rg_x, b_rg_x, rg_lambda, g_attn_out, g_rec_out, w_out, g_norm2, w_peer_q,
           peer_keys_1, peer_keys_2, peer_u, peer_v):
    for l in range(g_norm1.shape[0]):
        x = hybrid_layer(x, g_norm1[l], w_in[l], g_q_norm[l], g_k_norm[l], rel_bias[l],
                         conv_w[l], conv_b[l], w_rg_a[l], b_rg_a[l], w_rg_x[l], b_rg_x[l],
                         rg_lambda[l], g_attn_out[l], g_rec_out[l], w_out[l], g_norm2[l],
                         w_peer_q[l], peer_keys_1[l], peer_keys_2[l], peer_u[l], peer_v[l],
                         cfg=_CFG)
    return x
```

```python
import functools
import math

import jax
import jax.numpy as jnp
from jax import lax
from jax.experimental import pallas as pl
from jax.experimental.pallas import tpu as pltpu

EPS = 1e-6
CHUNK = 64
LEFT_CHUNKS = 8
LEFT = LEFT_CHUNKS * CHUNK
MAX_REL = 256
HEAD_DIM = 128
REC_BLOCK_DIM = 128
CONV_WIDTH = 4
RG_C = 8.0
PEER_HEADS = 8
PEER_N_KEYS = 128
PEER_TOPK = 16
LANES = 128
SUBLANES = 8
NEG = -1e30
LOG2_E = math.log2(math.e)
VMEM_LIMIT_BYTES = 60000 * 1024

F32 = jnp.float32
BF16 = jnp.bfloat16
_NT = (((1,), (1,)), ((), ()))


def _params(*sem):
    return pltpu.CompilerParams(dimension_semantics=sem, vmem_limit_bytes=VMEM_LIMIT_BYTES)


_GELU_C0 = math.sqrt(2.0 / math.pi)
_GELU_C1 = 0.044715 * _GELU_C0


def _gelu_tanh(x):
    return (0.5 * x) * (1.0 + jnp.tanh(x * (_GELU_C0 + _GELU_C1 * (x * x))))


def _sigmoid(x):
    return 0.5 + 0.5 * jnp.tanh(0.5 * x)


def _rmsnorm_kernel(x_ref, g_ref, o_ref):
    x = x_ref[...].astype(F32)
    ms = jnp.mean(x * x, axis=-1, keepdims=True)
    o_ref[...] = (x * lax.rsqrt(ms + EPS) * g_ref[...]).astype(o_ref.dtype)


def rmsnorm_rows(x, g, tm):
    t, d = x.shape
    return pl.pallas_call(
        _rmsnorm_kernel,
        grid=(t // tm,),
        in_specs=[pl.BlockSpec((tm, d), lambda i: (i, 0)),
                  pl.BlockSpec((1, d), lambda i: (0, 0))],
        out_specs=pl.BlockSpec((tm, d), lambda i: (i, 0)),
        out_shape=jax.ShapeDtypeStruct((t, d), BF16),
        compiler_params=_params("parallel"),
        name="rmsnorm_rows",
    )(x, g.reshape(1, d).astype(F32))


def _in_proj_kernel(a_ref, b_ref, g_ref, o_ref, *, n_norm_tiles):
    acc = jnp.dot(a_ref[...], b_ref[...], preferred_element_type=F32)
    j = pl.program_id(1)
    tn = o_ref.shape[1]

    @pl.when(j < n_norm_tiles)
    def _():
        g = g_ref[0]
        for c in range(tn // HEAD_DIM):
            cols = slice(c * HEAD_DIM, (c + 1) * HEAD_DIM)
            blk = acc[:, cols]
            ms = jnp.mean(blk * blk, axis=-1, keepdims=True)
            o_ref[:, cols] = (blk * lax.rsqrt(ms + EPS) * g).astype(o_ref.dtype)

    @pl.when(j >= n_norm_tiles)
    def _():
        o_ref[...] = acc.astype(o_ref.dtype)


def in_proj(h, w, gains, *, col0, n_cols, n_norm_cols, out_dtype, tm, tn):
    t, k = h.shape
    n_sec = gains.shape[0]
    sec_tiles = max(n_norm_cols // tn // n_sec, 1)
    jb0 = col0 // tn
    return pl.pallas_call(
        functools.partial(_in_proj_kernel, n_norm_tiles=n_norm_cols // tn),
        grid=(t // tm, n_cols // tn),
        in_specs=[pl.BlockSpec((tm, k), lambda i, j: (i, 0)),
                  pl.BlockSpec((k, tn), lambda i, j: (0, j + jb0)),
                  pl.BlockSpec((1, 1, HEAD_DIM),
                               lambda i, j: (jnp.minimum(j // sec_tiles, n_sec - 1), 0, 0))],
        out_specs=pl.BlockSpec((tm, tn), lambda i, j: (i, j)),
        out_shape=jax.ShapeDtypeStruct((t, n_cols), out_dtype),
        compiler_params=_params("parallel", "arbitrary"),
        name="in_proj",
    )(h, w, gains.reshape(n_sec, 1, HEAD_DIM).astype(F32))


def _rel_bias_diagonals(rel_bias, tq):
    w = LEFT + tq
    length = pl.cdiv(w + tq - 1, LANES) * LANES
    m = jnp.arange(length)
    d = jnp.where(m < w, m, m - length)
    rel = jnp.clip(LEFT - d, -MAX_REL, MAX_REL) + MAX_REL
    return rel_bias.astype(F32)[:, None, rel]


def _attn_kernel(*refs, n_prev, blocks_per_seq, scale):
    q_ref = refs[0]
    k_refs = refs[1:2 + n_prev]
    v_refs = refs[2 + n_prev:3 + 2 * n_prev]
    diag_ref, g_ref, o_ref, tbl_scr, o_scr = refs[3 + 2 * n_prev:]
    heads, tq, w = tbl_scr.shape
    i = pl.program_id(0)

    @pl.when(i == 0)
    def _():
        r = lax.broadcasted_iota(jnp.int32, (tq, w), 0)
        c = lax.broadcasted_iota(jnp.int32, (tq, w), 1)
        dchunk = r // CHUNK - c // CHUNK + LEFT_CHUNKS
        band = (dchunk >= 0) & (dchunk <= LEFT_CHUNKS)
        for hh in range(heads):
            diag = jnp.broadcast_to(diag_ref[hh], (tq, diag_ref.shape[2]))
            bias = pltpu.roll(diag, 0, axis=1, stride=1, stride_axis=0)[:, :w]
            tbl_scr[hh] = jnp.where(band, bias * LOG2_E, NEG)

    def heads_loop(mask_fn):
        for hh in range(heads):
            cols = slice(hh * HEAD_DIM, (hh + 1) * HEAD_DIM)
            k = jnp.concatenate([r[:, cols] for r in k_refs], axis=0)
            v = jnp.concatenate([r[:, cols] for r in v_refs], axis=0)
            s = lax.dot_general(q_ref[:, cols], k, _NT, preferred_element_type=F32)
            s = mask_fn(s * (scale * LOG2_E) + tbl_scr[hh])
            m = jnp.max(s, axis=-1, keepdims=True)
            p = jnp.exp2(s - m)
            l = jnp.sum(p, axis=-1, keepdims=True)
            o = jnp.dot(p.astype(v.dtype), v, preferred_element_type=F32)
            o_scr[:, cols] = o / l

    jb = i % blocks_per_seq

    @pl.when(jb < n_prev)
    def _():
        col = lax.broadcasted_iota(jnp.int32, (tq, w), 1)
        in_seq = col >= (n_prev - jb) * tq
        heads_loop(lambda s: jnp.where(in_seq, s, NEG))

    @pl.when(jb >= n_prev)
    def _():
        heads_loop(lambda s: s)

    attn = o_scr[...]
    ms = jnp.mean(attn * attn, axis=-1, keepdims=True)
    o_ref[...] = (attn * lax.rsqrt(ms + EPS) * g_ref[...]).astype(o_ref.dtype)


def band_attention(qkv, rel_bias, g_out, *, seq, n_heads, tq):
    t = qkv.shape[0]
    n_prev = LEFT // tq
    bps = seq // tq
    d_attn = n_heads * HEAD_DIM
    diags = _rel_bias_diagonals(rel_bias, tq)

    def win_spec(col_block, back):
        def idx(i):
            first = (i // bps) * bps
            return (jnp.maximum(i - back, first), col_block)
        return pl.BlockSpec((tq, d_attn), idx)

    backs = list(range(n_prev, -1, -1))
    in_specs = ([pl.BlockSpec((tq, d_attn), lambda i: (i, 0))]
                + [win_spec(1, b) for b in backs]
                + [win_spec(2, b) for b in backs]
                + [pl.BlockSpec(diags.shape, lambda i: (0, 0, 0)),
                   pl.BlockSpec((1, d_attn), lambda i: (0, 0))])
    n_win = n_prev + 1
    return pl.pallas_call(
        functools.partial(_attn_kernel, n_prev=n_prev, blocks_per_seq=bps,
                          scale=HEAD_DIM ** -0.5),
        grid=(t // tq,),
        in_specs=in_specs,
        out_specs=pl.BlockSpec((tq, d_attn), lambda i: (i, 0)),
        out_shape=jax.ShapeDtypeStruct((t, d_attn), BF16),
        scratch_shapes=[pltpu.VMEM((n_heads, tq, LEFT + tq), F32),
                        pltpu.VMEM((tq, d_attn), F32)],
        compiler_params=_params("arbitrary"),
        name="band_attention",
    )(*([qkv] * (1 + 2 * n_win)), diags, g_out.reshape(1, d_attn).astype(F32))


def _rec_kernel(x_ref, y_ref, cw_ref, cb_ref, wa_ref, wx_ref, ba_ref, bx_ref, lam_ref, g_ref,
                o_ref, xpad, hcar, rec_scr, *, lane_chunk):
    tt, c_all = x_ref.shape
    groups = tt // SUBLANES
    pad0 = SUBLANES - (CONV_WIDTH - 1)

    @pl.when(pl.program_id(1) == 0)
    def _():
        xpad[0:SUBLANES, :] = jnp.zeros((SUBLANES, c_all), F32)
        hcar[...] = jnp.zeros_like(hcar)

    xpad[SUBLANES:SUBLANES + tt, :] = x_ref[...]
    ss = jnp.zeros((tt, 1), F32)
    row = lax.broadcasted_iota(jnp.int32, (groups, SUBLANES, lane_chunk), 1)
    for cbk in range(c_all // lane_chunk):
        ls = slice(cbk * lane_chunk, (cbk + 1) * lane_chunk)
        xr = cb_ref[:, ls]
        for tap in range(CONV_WIDTH):
            xr = xr + xpad[pad0 + tap:pad0 + tap + tt, ls] * cw_ref[tap:tap + 1, ls]
        xr_b = xr.astype(BF16)
        pre_a, pre_x = [], []
        for n in range(lane_chunk // REC_BLOCK_DIM):
            nb = cbk * (lane_chunk // REC_BLOCK_DIM) + n
            xs = xr_b[:, n * REC_BLOCK_DIM:(n + 1) * REC_BLOCK_DIM]
            pre_a.append(jnp.dot(xs, wa_ref[nb], preferred_element_type=F32))
            pre_x.append(jnp.dot(xs, wx_ref[nb], preferred_element_type=F32))
        r = _sigmoid(jnp.concatenate(pre_a, axis=1) + ba_ref[:, ls])
        gate_i = _sigmoid(jnp.concatenate(pre_x, axis=1) + bx_ref[:, ls])
        z = -lam_ref[:, ls]
        softplus = jnp.maximum(z, 0.0) + jnp.log1p(jnp.exp(-jnp.abs(z)))
        log_a = (-RG_C * r) * softplus
        a = jnp.exp(log_a)
        m2 = -jnp.tanh(log_a) * (a * a + 1.0)
        mult = jnp.where(m2 > 0.0, m2 * lax.rsqrt(m2), 0.0)
        u = mult * (gate_i * xr)
        a3 = a.reshape(groups, SUBLANES, lane_chunk)
        u3 = u.reshape(groups, SUBLANES, lane_chunk)
        d = 1
        while d < SUBLANES:
            keep = row >= d
            a_sh = jnp.where(keep, pltpu.roll(a3, d, axis=1), 1.0)
            u_sh = jnp.where(keep, pltpu.roll(u3, d, axis=1), 0.0)
            u3 = a3 * u_sh + u3
            a3 = a3 * a_sh
            d *= 2
        h = hcar[:, ls]
        hs = []
        for gi in range(groups):
            hg = u3[gi] + a3[gi] * h
            hs.append(hg)
            h = hg[SUBLANES - 1:SUBLANES, :]
        hcar[:, ls] = h
        rec = jnp.concatenate(hs, axis=0) * _gelu_tanh(y_ref[:, ls])
        rec_scr[:, ls] = rec
        ss = ss + jnp.sum(rec * rec, axis=-1, keepdims=True)
    xpad[0:SUBLANES, :] = x_ref[tt - SUBLANES:tt, :]
    o_ref[...] = (rec_scr[...] * lax.rsqrt(ss / c_all + EPS) * g_ref[...]).astype(o_ref.dtype)


def rec_branch(xy, conv_w, conv_b, w_a, b_a, w_x, b_x, lam, g_out, *, batch, tt, lane_chunk):
    t, c2 = xy.shape
    c = c2 // 2
    nt = t // batch // tt
    row = lambda v: v.reshape(1, c).astype(F32)
    vec = pl.BlockSpec((1, c), lambda b, i: (0, 0))
    wspec = pl.BlockSpec(w_a.shape, lambda b, i: (0, 0, 0))
    return pl.pallas_call(
        functools.partial(_rec_kernel, lane_chunk=lane_chunk),
        grid=(batch, nt),
        in_specs=[pl.BlockSpec((tt, c), lambda b, i: (b * nt + i, 0)),
                  pl.BlockSpec((tt, c), lambda b, i: (b * nt + i, 1)),
                  pl.BlockSpec((CONV_WIDTH, c), lambda b, i: (0, 0)),
                  vec, wspec, wspec, vec, vec, vec, vec],
        out_specs=pl.BlockSpec((tt, c), lambda b, i: (b * nt + i, 0)),
        out_shape=jax.ShapeDtypeStruct((t, c), BF16),
        scratch_shapes=[pltpu.VMEM((SUBLANES + tt, c), F32),
                        pltpu.VMEM((1, c), F32),
                        pltpu.VMEM((tt, c), F32)],
        compiler_params=_params("parallel", "arbitrary"),
        name="rec_branch",
    )(xy, xy, conv_w.astype(F32), row(conv_b), w_a.astype(BF16), w_x.astype(BF16),
      row(b_a), row(b_x), row(lam), row(g_out))


def _out_proj_kernel(a1_ref, a2_ref, b1_ref, b2_ref, x_ref, o_ref):
    acc = jnp.dot(a1_ref[...], b1_ref[...], preferred_element_type=F32)
    acc = acc + jnp.dot(a2_ref[...], b2_ref[...], preferred_element_type=F32)
    o_ref[...] = x_ref[...] + acc


def out_proj(a1, a2, w, x, *, tm, tn):
    t, k1 = a1.shape
    n = w.shape[1]
    return pl.pallas_call(
        _out_proj_kernel,
        grid=(t // tm, n // tn),
        in_specs=[pl.BlockSpec((tm, k1), lambda i, j: (i, 0)),
                  pl.BlockSpec((tm, k1), lambda i, j: (i, 0)),
                  pl.BlockSpec((k1, tn), lambda i, j: (0, j)),
                  pl.BlockSpec((k1, tn), lambda i, j: (1, j)),
                  pl.BlockSpec((tm, tn), lambda i, j: (i, j))],
        out_specs=pl.BlockSpec((tm, tn), lambda i, j: (i, j)),
        out_shape=jax.ShapeDtypeStruct((t, n), F32),
        compiler_params=_params("parallel", "arbitrary"),
        name="out_proj",
    )(a1, a2, w, w, x)


def _peer_scores_kernel(a_ref, b_ref, keys_ref, st_ref):
    q = jnp.dot(a_ref[...], b_ref[...], preferred_element_type=F32).astype(BF16)
    for c in range(keys_ref.shape[0]):
        rows = slice(c * PEER_N_KEYS, (c + 1) * PEER_N_KEYS)
        st_ref[rows, :] = lax.dot_general(keys_ref[c], q[:, rows], _NT,
                                          preferred_element_type=F32)


def peer_scores(h, w_q, keys, *, tm, tn):
    t, k = h.shape
    n = w_q.shape[1]
    kb = tn // PEER_N_KEYS
    return pl.pallas_call(
        _peer_scores_kernel,
        grid=(t // tm, n // tn),
        in_specs=[pl.BlockSpec((tm, k), lambda i, j: (i, 0)),
                  pl.BlockSpec((k, tn), lambda i, j: (0, j)),
                  pl.BlockSpec((kb,) + keys.shape[1:], lambda i, j: (j, 0, 0))],
        out_specs=pl.BlockSpec((tn, tm), lambda i, j: (j, i)),
        out_shape=jax.ShapeDtypeStruct((n, t), F32),
        compiler_params=_params("parallel", "arbitrary"),
        name="peer_scores",
    )(h, w_q, keys)


_CAND_PAIRS = [(a, b) for a in range(PEER_TOPK + 1) for b in range(PEER_TOPK + 1)
               if (a + 1) * (b + 1) <= PEER_TOPK + 1]


def _top_values(s, count):
    rows = []
    for _ in range(count):
        m = jnp.max(s, axis=0, keepdims=True)
        rows.append(m)
        s = jnp.where(s == m, NEG, s)
    return rows


def _peer_topk_kernel(s_ref, e2_ref, thr_ref, hc_ref, thr_scr, hc_scr):
    nh, _, nk, _ = s_ref.shape

    def head(h, carry):
        s1 = s_ref[h, 0]
        s2 = s_ref[h, 1]
        t1 = _top_values(s1, PEER_TOPK + 1)
        t2 = _top_values(s2, PEER_TOPK + 1)
        cands = jnp.concatenate([t1[a] + t2[b] for a, b in _CAND_PAIRS], axis=0)
        top = _top_values(cands, PEER_TOPK + 1)
        tau = 0.5 * (top[PEER_TOPK - 1] + top[PEER_TOPK])
        z = jnp.sum(jnp.where(cands > tau, jnp.exp(cands - top[0]), 0.0), axis=0, keepdims=True)
        thr_scr[h] = tau - s1
        hc_scr[h] = (0.5 * jnp.exp(s1 - t1[0])) / z
        e2_ref[h] = pltpu.bitcast(jnp.exp(s2 - t2[0]).astype(BF16), jnp.uint32)
        return carry

    lax.fori_loop(0, nh, head, 0)
    for h in range(nh):
        for lc in range(thr_ref.shape[0]):
            ls = slice(lc * LANES, (lc + 1) * LANES)
            thr_ref[lc, pl.ds(h, nk, stride=nh), :] = thr_scr[h, :, ls]
            hc_ref[lc, pl.ds(h, nk, stride=nh), :] = hc_scr[h, :, ls]


def peer_topk(st, *, tt):
    nh, _, nk, t = st.shape
    key_major = pl.BlockSpec((tt // LANES, nk * nh, LANES), lambda i: (i, 0, 0))
    key_major_shape = jax.ShapeDtypeStruct((t // LANES, nk * nh, LANES), F32)
    return pl.pallas_call(
        _peer_topk_kernel,
        grid=(t // tt,),
        in_specs=[pl.BlockSpec((nh, 2, nk, tt), lambda i: (0, 0, 0, i))],
        out_specs=[pl.BlockSpec((nh, nk // 2, tt), lambda i: (0, 0, i)), key_major, key_major],
        out_shape=[jax.ShapeDtypeStruct((nh, nk // 2, t), jnp.uint32), key_major_shape,
                   key_major_shape],
        scratch_shapes=[pltpu.VMEM((nh, nk, tt), F32), pltpu.VMEM((nh, nk, tt), F32)],
        compiler_params=_params("parallel"),
        name="peer_topk",
    )(st)


def _peer_dense_kernel(h_ref, u_ref, v_ref, s2_ref, e2_ref, thr_ref, hc_ref, x_ref,
                       o_ref, act_scr, w_scr):
    te, tt = act_scr.shape
    nh, nk = s2_ref.shape[0], s2_ref.shape[1]

    @pl.when(pl.program_id(1) == 0)
    def _():
        o_ref[...] = x_ref[...]

    act_scr[...] = lax.dot_general(u_ref[...], h_ref[...], _NT, preferred_element_type=F32)
    for ii in range(te // nk):
        for lc in range(tt // LANES):
            ls = slice(lc * LANES, (lc + 1) * LANES)
            half_gate = None
            for h in range(nh):
                r = ii * nh + h
                half_c = jnp.broadcast_to(hc_ref[lc, r:r + 1, :], (nk, LANES)).astype(BF16)
                weight = pltpu.bitcast(e2_ref[h, :, ls], BF16) * half_c
                part = jnp.where(s2_ref[h, :, ls] >= thr_ref[lc, r:r + 1, :], weight,
                                 jnp.zeros_like(weight))
                half_gate = part if half_gate is None else half_gate + part
            a = act_scr[ii * nk:(ii + 1) * nk, ls]
            t = jnp.tanh(a * (_GELU_C0 + _GELU_C1 * (a * a)))
            w_tile = (half_gate * a) * (1.0 + t)
            w_scr[ls, ii * nk:(ii + 1) * nk] = w_tile.T.astype(w_scr.dtype)
    o_ref[...] += jnp.dot(w_scr[...], v_ref[...], preferred_element_type=F32)


def peer_dense(h, u, v, st, e2, thr, half_c, x, *, tt, te):
    t, d = h.shape
    e = u.shape[0]
    nh, _, nk, _ = st.shape
    once = pl.Buffered(1)
    key_rows = te // nk * nh
    return pl.pallas_call(
        _peer_dense_kernel,
        grid=(t // tt, e // te),
        in_specs=[pl.BlockSpec((tt, d), lambda i, j: (i, 0), pipeline_mode=once),
                  pl.BlockSpec((te, d), lambda i, j: (j, 0)),
                  pl.BlockSpec((te, d), lambda i, j: (j, 0)),
                  pl.BlockSpec((nh, None, nk, tt), lambda i, j: (0, 1, 0, i), pipeline_mode=once),
                  pl.BlockSpec((nh, nk // 2, tt), lambda i, j: (0, 0, i), pipeline_mode=once),
                  pl.BlockSpec((tt // LANES, key_rows, LANES), lambda i, j: (i, j, 0)),
                  pl.BlockSpec((tt // LANES, key_rows, LANES), lambda i, j: (i, j, 0)),
                  pl.BlockSpec((tt, d), lambda i, j: (i, 0))],
        out_specs=pl.BlockSpec((tt, d), lambda i, j: (i, 0)),
        out_shape=jax.ShapeDtypeStruct((t, d), F32),
        scratch_shapes=[pltpu.VMEM((te, tt), F32), pltpu.VMEM((tt, te), BF16)],
        compiler_params=_params("parallel", "arbitrary"),
        name="peer_dense",
    )(h, u, v, st, e2, thr, half_c, x)


def hybrid_layer(x, g_norm1, w_in, g_q_norm, g_k_norm, rel_bias, conv_w, conv_b,
                 w_rg_a, b_rg_a, w_rg_x, b_rg_x, rg_lambda, g_attn_out, g_rec_out,
                 w_out, g_norm2, w_peer_q, peer_keys_1, peer_keys_2, peer_u, peer_v,
                 *, cfg):
    b, s, d = x.shape
    t = b * s
    d_attn = g_attn_out.shape[0]
    d_rec = g_rec_out.shape[0]
    n_heads = d_attn // HEAD_DIM
    x2 = x.reshape(t, d)

    h1 = rmsnorm_rows(x2, g_norm1, cfg["norm_tm"])
    w_in_b = w_in.astype(BF16)
    qkv = in_proj(h1, w_in_b, jnp.stack([g_q_norm, g_k_norm]), col0=0, n_cols=3 * d_attn,
                  n_norm_cols=2 * d_attn, out_dtype=BF16, tm=cfg["mm_tm"], tn=cfg["mm_tn"])
    xy = in_proj(h1, w_in_b, jnp.ones((1, HEAD_DIM), F32), col0=3 * d_attn, n_cols=2 * d_rec,
                 n_norm_cols=0, out_dtype=F32, tm=cfg["mm_tm"], tn=cfg["mm_tn"])

    attn_n = band_attention(qkv, rel_bias, g_attn_out, seq=s, n_heads=n_heads,
                            tq=cfg["attn_tq"])
    rec_n = rec_branch(xy, conv_w, conv_b, w_rg_a, b_rg_a, w_rg_x, b_rg_x, rg_lambda, g_rec_out,
                       batch=b, tt=cfg["rec_tt"], lane_chunk=cfg["rec_lanes"])
    x1 = out_proj(attn_n, rec_n, w_out.astype(BF16), x2, tm=cfg["mm_tm"], tn=cfg["mm_tn"])

    h2 = rmsnorm_rows(x1, g_norm2, cfg["norm_tm"])
    nh, nk, half = peer_keys_1.shape
    keys = jnp.stack([peer_keys_1, peer_keys_2], axis=1).reshape(2 * nh, nk, half).astype(BF16)
    st = peer_scores(h2, w_peer_q.astype(BF16), keys, tm=cfg["mm_tm"], tn=cfg["mm_tn"])
    st = st.reshape(nh, 2, nk, t)
    e2, thr, half_c = peer_topk(st, tt=cfg["topk_tt"])
    out = peer_dense(h2, peer_u.astype(BF16), peer_v.astype(BF16), st, e2, thr, half_c, x1,
                     tt=cfg["peer_tt"], te=cfg["peer_te"])
    return out.reshape(b, s, d)


_CFG = dict(norm_tm=256, mm_tm=1024, mm_tn=1024, attn_tq=256, rec_tt=256, rec_lanes=512,
            topk_tt=256, peer_tt=512, peer_te=512)


def kernel(x, g_norm1, w_in, g_q_norm, g_k_norm, rel_bias, conv_w, conv_b, w_rg_a, b_rg_a,
           w_rg_x, b_rg_x, rg_lambda, g_attn_out, g_rec_out, w_out, g_norm2, w_peer_q,
           peer_keys_1, peer_keys_2, peer_u, peer_v):
    for l in range(g_norm1.shape[0]):
        x = hybrid_layer(x, g_norm1[l], w_in[l], g_q_norm[l], g_k_norm[l], rel_bias[l],
                         conv_w[l], conv_b[l], w_rg_a[l], b_rg_a[l], w_rg_x[l], b_rg_x[l],
                         rg_lambda[l], g_attn_out[l], g_rec_out[l], w_out[l], g_norm2[l],
                         w_peer_q[l], peer_keys_1[l], peer_keys_2[l], peer_u[l], peer_v[l],
                         cfg=_CFG)
    return x
```

```python
import functools
import math

import jax
import jax.numpy as jnp
from jax import lax
from jax.experimental import pallas as pl
from jax.experimental.pallas import tpu as pltpu

EPS = 1e-6
CHUNK = 64
LEFT_CHUNKS = 8
LEFT = LEFT_CHUNKS * CHUNK
MAX_REL = 256
HEAD_DIM = 128
REC_BLOCK_DIM = 128
CONV_WIDTH = 4
RG_C = 8.0
PEER_HEADS = 8
PEER_N_KEYS = 128
PEER_TOPK = 16
LANES = 128
SUBLANES = 8
NEG = -1e30
LOG2_E = math.log2(math.e)
VMEM_LIMIT_BYTES = 60000 * 1024

F32 = jnp.float32
BF16 = jnp.bfloat16
_NT = (((1,), (1,)), ((), ()))


def _params(*sem):
    return pltpu.CompilerParams(dimension_semantics=sem, vmem_limit_bytes=VMEM_LIMIT_BYTES)


_GELU_C0 = math.sqrt(2.0 / math.pi)
_GELU_C1 = 0.044715 * _GELU_C0


def _gelu_tanh(x):
    return (0.5 * x) * (1.0 + jnp.tanh(x * (_GELU_C0 + _GELU_C1 * (x * x))))


def _sigmoid(x):
    return 0.5 + 0.5 * jnp.tanh(0.5 * x)


def _rmsnorm_kernel(x_ref, g_ref, o_ref):
    x = x_ref[...].astype(F32)
    ms = jnp.mean(x * x, axis=-1, keepdims=True)
    o_ref[...] = (x * lax.rsqrt(ms + EPS) * g_ref[...]).astype(o_ref.dtype)


def rmsnorm_rows(x, g, tm):
    t, d = x.shape
    return pl.pallas_call(
        _rmsnorm_kernel,
        grid=(t // tm,),
        in_specs=[pl.BlockSpec((tm, d), lambda i: (i, 0)),
                  pl.BlockSpec((1, d), lambda i: (0, 0))],
        out_specs=pl.BlockSpec((tm, d), lambda i: (i, 0)),
        out_shape=jax.ShapeDtypeStruct((t, d), BF16),
        compiler_params=_params("parallel"),
        name="rmsnorm_rows",
    )(x, g.reshape(1, d).astype(F32))


def _in_proj_kernel(a_ref, b_ref, g_ref, o_ref, *, n_norm_tiles):
    acc = jnp.dot(a_ref[...], b_ref[...], preferred_element_type=F32)
    j = pl.program_id(1)
    tn = o_ref.shape[1]

    @pl.when(j < n_norm_tiles)
    def _():
        g = g_ref[0]
        for c in range(tn // HEAD_DIM):
            cols = slice(c * HEAD_DIM, (c + 1) * HEAD_DIM)
            blk = acc[:, cols]
            ms = jnp.mean(blk * blk, axis=-1, keepdims=True)
            o_ref[:, cols] = (blk * lax.rsqrt(ms + EPS) * g).astype(o_ref.dtype)

    @pl.when(j >= n_norm_tiles)
    def _():
        o_ref[...] = acc.astype(o_ref.dtype)


def in_proj(h, w, gains, *, col0, n_cols, n_norm_cols, out_dtype, tm, tn):
    t, k = h.shape
    n_sec = gains.shape[0]
    sec_tiles = max(n_norm_cols // tn // n_sec, 1)
    jb0 = col0 // tn
    return pl.pallas_call(
        functools.partial(_in_proj_kernel, n_norm_tiles=n_norm_cols // tn),
        grid=(t // tm, n_cols // tn),
        in_specs=[pl.BlockSpec((tm, k), lambda i, j: (i, 0)),
                  pl.BlockSpec((k, tn), lambda i, j: (0, j + jb0)),
                  pl.BlockSpec((1, 1, HEAD_DIM),
                               lambda i, j: (jnp.minimum(j // sec_tiles, n_sec - 1), 0, 0))],
        out_specs=pl.BlockSpec((tm, tn), lambda i, j: (i, j)),
        out_shape=jax.ShapeDtypeStruct((t, n_cols), out_dtype),
        compiler_params=_params("parallel", "arbitrary"),
        name="in_proj",
    )(h, w, gains.reshape(n_sec, 1, HEAD_DIM).astype(F32))


def _rel_bias_diagonals(rel_bias, tq):
    w = LEFT + tq
    length = pl.cdiv(w + tq - 1, LANES) * LANES
    m = jnp.arange(length)
    d = jnp.where(m < w, m, m - length)
    rel = jnp.clip(LEFT - d, -MAX_REL, MAX_REL) + MAX_REL
    return rel_bias.astype(F32)[:, None, rel]


def _attn_kernel(*refs, n_prev, blocks_per_seq, scale):
    q_ref = refs[0]
    k_refs = refs[1:2 + n_prev]
    v_refs = refs[2 + n_prev:3 + 2 * n_prev]
    diag_ref, g_ref, o_ref, tbl_scr, o_scr = refs[3 + 2 * n_prev:]
    heads, tq, w = tbl_scr.shape
    i = pl.program_id(0)

    @pl.when(i == 0)
    def _():
        r = lax.broadcasted_iota(jnp.int32, (tq, w), 0)
        c = lax.broadcasted_iota(jnp.int32, (tq, w), 1)
        dchunk = r // CHUNK - c // CHUNK + LEFT_CHUNKS
        band = (dchunk >= 0) & (dchunk <= LEFT_CHUNKS)
        for hh in range(heads):
            diag = jnp.broadcast_to(diag_ref[hh], (tq, diag_ref.shape[2]))
            bias = pltpu.roll(diag, 0, axis=1, stride=1, stride_axis=0)[:, :w]
            tbl_scr[hh] = jnp.where(band, bias * LOG2_E, NEG)

    def heads_loop(mask_fn):
        for hh in range(heads):
            cols = slice(hh * HEAD_DIM, (hh + 1) * HEAD_DIM)
            k = jnp.concatenate([r[:, cols] for r in k_refs], axis=0)
            v = jnp.concatenate([r[:, cols] for r in v_refs], axis=0)
            s = lax.dot_general(q_ref[:, cols], k, _NT, preferred_element_type=F32)
            s = mask_fn(s * (scale * LOG2_E) + tbl_scr[hh])
            m = jnp.max(s, axis=-1, keepdims=True)
            p = jnp.exp2(s - m)
            l = jnp.sum(p, axis=-1, keepdims=True)
            o = jnp.dot(p.astype(v.dtype), v, preferred_element_type=F32)
            o_scr[:, cols] = o / l

    jb = i % blocks_per_seq

    @pl.when(jb < n_prev)
    def _():
        col = lax.broadcasted_iota(jnp.int32, (tq, w), 1)
        in_seq = col >= (n_prev - jb) * tq
        heads_loop(lambda s: jnp.where(in_seq, s, NEG))

    @pl.when(jb >= n_prev)
    def _():
        heads_loop(lambda s: s)

    attn = o_scr[...]
    ms = jnp.mean(attn * attn, axis=-1, keepdims=True)
    o_ref[...] = (attn * lax.rsqrt(ms + EPS) * g_ref[...]).astype(o_ref.dtype)


def band_attention(qkv, rel_bias, g_out, *, seq, n_heads, tq):
    t = qkv.shape[0]
    n_prev = LEFT // tq
    bps = seq // tq
    d_attn = n_heads * HEAD_DIM
    diags = _rel_bias_diagonals(rel_bias, tq)

    def win_spec(col_block, back):
        def idx(i):
            first = (i // bps) * bps
            return (jnp.maximum(i - back, first), col_block)
        return pl.BlockSpec((tq, d_attn), idx)

    backs = list(range(n_prev, -1, -1))
    in_specs = ([pl.BlockSpec((tq, d_attn), lambda i: (i, 0))]
                + [win_spec(1, b) for b in backs]
                + [win_spec(2, b) for b in backs]
                + [pl.BlockSpec(diags.shape, lambda i: (0, 0, 0)),
                   pl.BlockSpec((1, d_attn), lambda i: (0, 0))])
    n_win = n_prev + 1
    return pl.pallas_call(
        functools.partial(_attn_kernel, n_prev=n_prev, blocks_per_seq=bps,
                          scale=HEAD_DIM ** -0.5),
        grid=(t // tq,),
        in_specs=in_specs,
        out_specs=pl.BlockSpec((tq, d_attn), lambda i: (i, 0)),
        out_shape=jax.ShapeDtypeStruct((t, d_attn), BF16),
        scratch_shapes=[pltpu.VMEM((n_heads, tq, LEFT + tq), F32),
                        pltpu.VMEM((tq, d_attn), F32)],
        compiler_params=_params("arbitrary"),
        name="band_attention",
    )(*([qkv] * (1 + 2 * n_win)), diags, g_out.reshape(1, d_attn).astype(F32))


def _rec_kernel(x_ref, y_ref, cw_ref, cb_ref, wa_ref, wx_ref, ba_ref, bx_ref, lam_ref, g_ref,
                o_ref, xpad, hcar, rec_scr, *, lane_chunk):
    tt, c_all = x_ref.shape
    groups = tt // SUBLANES
    pad0 = SUBLANES - (CONV_WIDTH - 1)

    @pl.when(pl.program_id(1) == 0)
    def _():
        xpad[0:SUBLANES, :] = jnp.zeros((SUBLANES, c_all), F32)
        hcar[...] = jnp.zeros_like(hcar)

    xpad[SUBLANES:SUBLANES + tt, :] = x_ref[...]
    ss = jnp.zeros((tt, 1), F32)
    row = lax.broadcasted_iota(jnp.int32, (groups, SUBLANES, lane_chunk), 1)
    for cbk in range(c_all // lane_chunk):
        ls = slice(cbk * lane_chunk, (cbk + 1) * lane_chunk)
        xr = cb_ref[:, ls]
        for tap in range(CONV_WIDTH):
            xr = xr + xpad[pad0 + tap:pad0 + tap + tt, ls] * cw_ref[tap:tap + 1, ls]
        xr_b = xr.astype(BF16)
        pre_a, pre_x = [], []
        for n in range(lane_chunk // REC_BLOCK_DIM):
            nb = cbk * (lane_chunk // REC_BLOCK_DIM) + n
            xs = xr_b[:, n * REC_BLOCK_DIM:(n + 1) * REC_BLOCK_DIM]
            pre_a.append(jnp.dot(xs, wa_ref[nb], preferred_element_type=F32))
            pre_x.append(jnp.dot(xs, wx_ref[nb], preferred_element_type=F32))
        r = _sigmoid(jnp.concatenate(pre_a, axis=1) + ba_ref[:, ls])
        gate_i = _sigmoid(jnp.concatenate(pre_x, axis=1) + bx_ref[:, ls])
        z = -lam_ref[:, ls]
        softplus = jnp.maximum(z, 0.0) + jnp.log1p(jnp.exp(-jnp.abs(z)))
        log_a = (-RG_C * r) * softplus
        a = jnp.exp(log_a)
        m2 = -jnp.tanh(log_a) * (a * a + 1.0)
        mult = jnp.where(m2 > 0.0, m2 * lax.rsqrt(m2), 0.0)
        u = mult * (gate_i * xr)
        a3 = a.reshape(groups, SUBLANES, lane_chunk)
        u3 = u.reshape(groups, SUBLANES, lane_chunk)
        d = 1
        while d < SUBLANES:
            keep = row >= d
            a_sh = jnp.where(keep, pltpu.roll(a3, d, axis=1), 1.0)
            u_sh = jnp.where(keep, pltpu.roll(u3, d, axis=1), 0.0)
            u3 = a3 * u_sh + u3
            a3 = a3 * a_sh
            d *= 2
        h = hcar[:, ls]
        hs = []
        for gi in range(groups):
            hg = u3[gi] + a3[gi] * h
            hs.append(hg)
            h = hg[SUBLANES - 1:SUBLANES, :]
        hcar[:, ls] = h
        rec = jnp.concatenate(hs, axis=0) * _gelu_tanh(y_ref[:, ls])
        rec_scr[:, ls] = rec
        ss = ss + jnp.sum(rec * rec, axis=-1, keepdims=True)
    xpad[0:SUBLANES, :] = x_ref[tt - SUBLANES:tt, :]
    o_ref[...] = (rec_scr[...] * lax.rsqrt(ss / c_all + EPS) * g_ref[...]).astype(o_ref.dtype)


def rec_branch(xy, conv_w, conv_b, w_a, b_a, w_x, b_x, lam, g_out, *, batch, tt, lane_chunk):
    t, c2 = xy.shape
    c = c2 // 2
    nt = t // batch // tt
    row = lambda v: v.reshape(1, c).astype(F32)
    vec = pl.BlockSpec((1, c), lambda b, i: (0, 0))
    wspec = pl.BlockSpec(w_a.shape, lambda b, i: (0, 0, 0))
    return pl.pallas_call(
        functools.partial(_rec_kernel, lane_chunk=lane_chunk),
        grid=(batch, nt),
        in_specs=[pl.BlockSpec((tt, c), lambda b, i: (b * nt + i, 0)),
                  pl.BlockSpec((tt, c), lambda b, i: (b * nt + i, 1)),
                  pl.BlockSpec((CONV_WIDTH, c), lambda b, i: (0, 0)),
                  vec, wspec, wspec, vec, vec, vec, vec],
        out_specs=pl.BlockSpec((tt, c), lambda b, i: (b * nt + i, 0)),
        out_shape=jax.ShapeDtypeStruct((t, c), BF16),
        scratch_shapes=[pltpu.VMEM((SUBLANES + tt, c), F32),
                        pltpu.VMEM((1, c), F32),
                        pltpu.VMEM((tt, c), F32)],
        compiler_params=_params("parallel", "arbitrary"),
        name="rec_branch",
    )(xy, xy, conv_w.astype(F32), row(conv_b), w_a.astype(BF16), w_x.astype(BF16),
      row(b_a), row(b_x), row(lam), row(g_out))


def _out_proj_kernel(a1_ref, a2_ref, b1_ref, b2_ref, x_ref, o_ref):
    acc = jnp.dot(a1_ref[...], b1_ref[...], preferred_element_type=F32)
    acc = acc + jnp.dot(a2_ref[...], b2_ref[...], preferred_element_type=F32)
    o_ref[...] = x_ref[...] + acc


def out_proj(a1, a2, w, x, *, tm, tn):
    t, k1 = a1.shape
    n = w.shape[1]
    return pl.pallas_call(
        _out_proj_kernel,
        grid=(t // tm, n // tn),
        in_specs=[pl.BlockSpec((tm, k1), lambda i, j: (i, 0)),
                  pl.BlockSpec((tm, k1), lambda i, j: (i, 0)),
                  pl.BlockSpec((k1, tn), lambda i, j: (0, j)),
                  pl.BlockSpec((k1, tn), lambda i, j: (1, j)),
                  pl.BlockSpec((tm, tn), lambda i, j: (i, j))],
        out_specs=pl.BlockSpec((tm, tn), lambda i, j: (i, j)),
        out_shape=jax.ShapeDtypeStruct((t, n), F32),
        compiler_params=_params("parallel", "arbitrary"),
        name="out_proj",
    )(a1, a2, w, w, x)


def _peer_scores_kernel(a_ref, b_ref, keys_ref, st_ref):
    q = jnp.dot(a_ref[...], b_ref[...], preferred_element_type=F32).astype(BF16)
    for c in range(keys_ref.shape[0]):
        rows = slice(c * PEER_N_KEYS, (c + 1) * PEER_N_KEYS)
        st_ref[rows, :] = lax.dot_general(keys_ref[c], q[:, rows], _NT,
                                          preferred_element_type=F32)


def peer_scores(h, w_q, keys, *, tm, tn):
    t, k = h.shape
    n = w_q.shape[1]
    kb = tn // PEER_N_KEYS
    return pl.pallas_call(
        _peer_scores_kernel,
        grid=(t // tm, n // tn),
        in_specs=[pl.BlockSpec((tm, k), lambda i, j: (i, 0)),
                  pl.BlockSpec((k, tn), lambda i, j: (0, j)),
                  pl.BlockSpec((kb,) + keys.shape[1:], lambda i, j: (j, 0, 0))],
        out_specs=pl.BlockSpec((tn, tm), lambda i, j: (j, i)),
        out_shape=jax.ShapeDtypeStruct((n, t), F32),
        compiler_params=_params("parallel", "arbitrary"),
        name="peer_scores",
    )(h, w_q, keys)


_CAND_PAIRS = [(a, b) for a in range(PEER_TOPK + 1) for b in range(PEER_TOPK + 1)
               if (a + 1) * (b + 1) <= PEER_TOPK + 1]


def _sorting_network(n):
    pairs = []
    p = 1
    while p < n:
        k = p
        while k >= 1:
            for j in range(k % p, n - k, 2 * k):
                for i in range(min(k, n - j - k)):
                    if (i + j) // (2 * p) == (i + j + k) // (2 * p):
                        pairs.append((i + j, i + j + k))
            k //= 2
        p *= 2
    return pairs


def _top_values(s, count):
    groups = s.shape[0] // SUBLANES
    v = [s[g * SUBLANES:(g + 1) * SUBLANES] for g in range(groups)]
    for a, b in _sorting_network(groups):
        v[a], v[b] = jnp.maximum(v[a], v[b]), jnp.minimum(v[a], v[b])
    rows = []
    for k in range(count):
        m = jnp.max(v[0], axis=0, keepdims=True)
        rows.append(m)
        live = min(groups, count - 1 - k)
        if live:
            held = v[0] == m
            for g in range(live):
                v[g] = jnp.where(held, v[g + 1] if g + 1 < groups else NEG, v[g])
    return rows


def _peer_topk_kernel(s_ref, e2_ref, thr_ref, hc_ref, thr_scr, hc_scr):
    nh, _, nk, _ = s_ref.shape

    def head(h, carry):
        s1 = s_ref[h, 0]
        s2 = s_ref[h, 1]
        t1 = _top_values(s1, PEER_TOPK + 1)
        t2 = _top_values(s2, PEER_TOPK + 1)
        pad_rows = pl.next_power_of_2(pl.cdiv(len(_CAND_PAIRS), SUBLANES)) * SUBLANES
        cands = jnp.concatenate(
            [t1[a] + t2[b] for a, b in _CAND_PAIRS]
            + [jnp.full((pad_rows - len(_CAND_PAIRS), s1.shape[1]), NEG, F32)], axis=0)
        top = _top_values(cands, PEER_TOPK + 1)
        tau = 0.5 * (top[PEER_TOPK - 1] + top[PEER_TOPK])
        z = jnp.sum(jnp.where(cands > tau, jnp.exp(cands - top[0]), 0.0), axis=0, keepdims=True)
        thr_scr[h] = tau - s1
        hc_scr[h] = (0.5 * jnp.exp(s1 - t1[0])) / z
        e2_ref[h] = pltpu.bitcast(jnp.exp(s2 - t2[0]).astype(BF16), jnp.uint32)
        return carry

    lax.fori_loop(0, nh, head, 0)
    for h in range(nh):
        for lc in range(thr_ref.shape[0]):
            ls = slice(lc * LANES, (lc + 1) * LANES)
            thr_ref[lc, pl.ds(h, nk, stride=nh), :] = thr_scr[h, :, ls]
            hc_ref[lc, pl.ds(h, nk, stride=nh), :] = hc_scr[h, :, ls]


def peer_topk(st, *, tt):
    nh, _, nk, t = st.shape
    key_major = pl.BlockSpec((tt // LANES, nk * nh, LANES), lambda i: (i, 0, 0))
    key_major_shape = jax.ShapeDtypeStruct((t // LANES, nk * nh, LANES), F32)
    return pl.pallas_call(
        _peer_topk_kernel,
        grid=(t // tt,),
        in_specs=[pl.BlockSpec((nh, 2, nk, tt), lambda i: (0, 0, 0, i))],
        out_specs=[pl.BlockSpec((nh, nk // 2, tt), lambda i: (0, 0, i)), key_major, key_major],
        out_shape=[jax.ShapeDtypeStruct((nh, nk // 2, t), jnp.uint32), key_major_shape,
                   key_major_shape],
        scratch_shapes=[pltpu.VMEM((nh, nk, tt), F32), pltpu.VMEM((nh, nk, tt), F32)],
        compiler_params=_params("parallel"),
        name="peer_topk",
    )(st)


def _peer_dense_kernel(h_ref, u_ref, v_ref, s2_ref, e2_ref, thr_ref, hc_ref, x_ref,
                       o_ref, act_scr, w_scr):
    te, tt = act_scr.shape
    nh, nk = s2_ref.shape[0], s2_ref.shape[1]

    @pl.when(pl.program_id(1) == 0)
    def _():
        o_ref[...] = x_ref[...]

    act_scr[...] = lax.dot_general(u_ref[...], h_ref[...], _NT, preferred_element_type=F32)
    for ii in range(te // nk):
        for lc in range(tt // LANES):
            ls = slice(lc * LANES, (lc + 1) * LANES)
            half_gate = None
            for h in range(nh):
                r = ii * nh + h
                half_c = jnp.broadcast_to(hc_ref[lc, r:r + 1, :], (nk, LANES)).astype(BF16)
                weight = pltpu.bitcast(e2_ref[h, :, ls], BF16) * half_c
                part = jnp.where(s2_ref[h, :, ls] >= thr_ref[lc, r:r + 1, :], weight,
                                 jnp.zeros_like(weight))
                half_gate = part if half_gate is None else half_gate + part
            a = act_scr[ii * nk:(ii + 1) * nk, ls]
            t = jnp.tanh(a * (_GELU_C0 + _GELU_C1 * (a * a)))
            w_tile = (half_gate * a) * (1.0 + t)
            w_scr[ls, ii * nk:(ii + 1) * nk] = w_tile.T.astype(w_scr.dtype)
    o_ref[...] += jnp.dot(w_scr[...], v_ref[...], preferred_element_type=F32)


def peer_dense(h, u, v, st, e2, thr, half_c, x, *, tt, te):
    t, d = h.shape
    e = u.shape[0]
    nh, _, nk, _ = st.shape
    once = pl.Buffered(1)
    key_rows = te // nk * nh
    return pl.pallas_call(
        _peer_dense_kernel,
        grid=(t // tt, e // te),
        in_specs=[pl.BlockSpec((tt, d), lambda i, j: (i, 0), pipeline_mode=once),
                  pl.BlockSpec((te, d), lambda i, j: (j, 0)),
                  pl.BlockSpec((te, d), lambda i, j: (j, 0)),
                  pl.BlockSpec((nh, None, nk, tt), lambda i, j: (0, 1, 0, i), pipeline_mode=once),
                  pl.BlockSpec((nh, nk // 2, tt), lambda i, j: (0, 0, i), pipeline_mode=once),
                  pl.BlockSpec((tt // LANES, key_rows, LANES), lambda i, j: (i, j, 0)),
                  pl.BlockSpec((tt // LANES, key_rows, LANES), lambda i, j: (i, j, 0)),
                  pl.BlockSpec((tt, d), lambda i, j: (i, 0))],
        out_specs=pl.BlockSpec((tt, d), lambda i, j: (i, 0)),
        out_shape=jax.ShapeDtypeStruct((t, d), F32),
        scratch_shapes=[pltpu.VMEM((te, tt), F32), pltpu.VMEM((tt, te), BF16)],
        compiler_params=_params("parallel", "arbitrary"),
        name="peer_dense",
    )(h, u, v, st, e2, thr, half_c, x)


def hybrid_layer(x, g_norm1, w_in, g_q_norm, g_k_norm, rel_bias, conv_w, conv_b,
                 w_rg_a, b_rg_a, w_rg_x, b_rg_x, rg_lambda, g_attn_out, g_rec_out,
                 w_out, g_norm2, w_peer_q, peer_keys_1, peer_keys_2, peer_u, peer_v,
                 *, cfg):
    b, s, d = x.shape
    t = b * s
    d_attn = g_attn_out.shape[0]
    d_rec = g_rec_out.shape[0]
    n_heads = d_attn // HEAD_DIM
    x2 = x.reshape(t, d)

    h1 = rmsnorm_rows(x2, g_norm1, cfg["norm_tm"])
    w_in_b = w_in.astype(BF16)
    qkv = in_proj(h1, w_in_b, jnp.stack([g_q_norm, g_k_norm]), col0=0, n_cols=3 * d_attn,
                  n_norm_cols=2 * d_attn, out_dtype=BF16, tm=cfg["mm_tm"], tn=cfg["mm_tn"])
    xy = in_proj(h1, w_in_b, jnp.ones((1, HEAD_DIM), F32), col0=3 * d_attn, n_cols=2 * d_rec,
                 n_norm_cols=0, out_dtype=F32, tm=cfg["mm_tm"], tn=cfg["mm_tn"])

    attn_n = band_attention(qkv, rel_bias, g_attn_out, seq=s, n_heads=n_heads,
                            tq=cfg["attn_tq"])
    rec_n = rec_branch(xy, conv_w, conv_b, w_rg_a, b_rg_a, w_rg_x, b_rg_x, rg_lambda, g_rec_out,
                       batch=b, tt=cfg["rec_tt"], lane_chunk=cfg["rec_lanes"])
    x1 = out_proj(attn_n, rec_n, w_out.astype(BF16), x2, tm=cfg["mm_tm"], tn=cfg["mm_tn"])

    h2 = rmsnorm_rows(x1, g_norm2, cfg["norm_tm"])
    nh, nk, half = peer_keys_1.shape
    keys = jnp.stack([peer_keys_1, peer_keys_2], axis=1).reshape(2 * nh, nk, half).astype(BF16)
    st = peer_scores(h2, w_peer_q.astype(BF16), keys, tm=cfg["mm_tm"], tn=cfg["mm_tn"])
    st = st.reshape(nh, 2, nk, t)
    e2, thr, half_c = peer_topk(st, tt=cfg["topk_tt"])
    out = peer_dense(h2, peer_u.astype(BF16), peer_v.astype(BF16), st, e2, thr, half_c, x1,
                     tt=cfg["peer_tt"], te=cfg["peer_te"])
    return out.reshape(b, s, d)


_CFG = dict(norm_tm=256, mm_tm=1024, mm_tn=1024, attn_tq=256, rec_tt=256, rec_lanes=512,
            topk_tt=256, peer_tt=512, peer_te=512)


def kernel(x, g_norm1, w_in, g_q_norm, g_k_norm, rel_bias, conv_w, conv_b, w_rg_a, b_rg_a,
           w_rg_x, b_rg_x, rg_lambda, g_attn_out, g_rec_out, w_out, g_norm2, w_peer_q,
           peer_keys_1, peer_keys_2, peer_u, peer_v):
    for l in range(g_norm1.shape[0]):
        x = hybrid_layer(x, g_norm1[l], w_in[l], g_q_norm[l], g_k_norm[l], rel_bias[l],
                         conv_w[l], conv_b[l], w_rg_a[l], b_rg_a[l], w_rg_x[l], b_rg_x[l],
                         rg_lambda[l], g_attn_out[l], g_rec_out[l], w_out[l], g_norm2[l],
                         w_peer_q[l], peer_keys_1[l], peer_keys_2[l], peer_u[l], peer_v[l],
                         cfg=_CFG)
    return x
```

```python
import functools
import math

import jax
import jax.numpy as jnp
from jax import lax
from jax.experimental import pallas as pl
from jax.experimental.pallas import tpu as pltpu

EPS = 1e-6
CHUNK = 64
LEFT_CHUNKS = 8
LEFT = LEFT_CHUNKS * CHUNK
MAX_REL = 256
HEAD_DIM = 128
REC_BLOCK_DIM = 128
CONV_WIDTH = 4
RG_C = 8.0
PEER_HEADS = 8
PEER_N_KEYS = 128
PEER_TOPK = 16
LANES = 128
SUBLANES = 8
NEG = -1e30
LOG2_E = math.log2(math.e)
VMEM_LIMIT_BYTES = 60000 * 1024

F32 = jnp.float32
BF16 = jnp.bfloat16
_NT = (((1,), (1,)), ((), ()))


def _params(*sem):
    return pltpu.CompilerParams(dimension_semantics=sem, vmem_limit_bytes=VMEM_LIMIT_BYTES)


_GELU_C0 = math.sqrt(2.0 / math.pi)
_GELU_C1 = 0.044715 * _GELU_C0


def _gelu_tanh(x):
    return (0.5 * x) * (1.0 + jnp.tanh(x * (_GELU_C0 + _GELU_C1 * (x * x))))


def _sigmoid(x):
    return 0.5 + 0.5 * jnp.tanh(0.5 * x)


def _rmsnorm_kernel(x_ref, g_ref, o_ref):
    x = x_ref[...].astype(F32)
    ms = jnp.mean(x * x, axis=-1, keepdims=True)
    o_ref[...] = (x * lax.rsqrt(ms + EPS) * g_ref[...]).astype(o_ref.dtype)


def rmsnorm_rows(x, g, tm):
    t, d = x.shape
    return pl.pallas_call(
        _rmsnorm_kernel,
        grid=(t // tm,),
        in_specs=[pl.BlockSpec((tm, d), lambda i: (i, 0)),
                  pl.BlockSpec((1, d), lambda i: (0, 0))],
        out_specs=pl.BlockSpec((tm, d), lambda i: (i, 0)),
        out_shape=jax.ShapeDtypeStruct((t, d), BF16),
        compiler_params=_params("parallel"),
        name="rmsnorm_rows",
    )(x, g.reshape(1, d).astype(F32))


def _in_proj_kernel(a_ref, b_ref, g_ref, o_ref, *, n_norm_tiles):
    acc = jnp.dot(a_ref[...], b_ref[...], preferred_element_type=F32)
    j = pl.program_id(1)
    tn = o_ref.shape[1]

    @pl.when(j < n_norm_tiles)
    def _():
        g = g_ref[0]
        for c in range(tn // HEAD_DIM):
            cols = slice(c * HEAD_DIM, (c + 1) * HEAD_DIM)
            blk = acc[:, cols]
            ms = jnp.mean(blk * blk, axis=-1, keepdims=True)
            o_ref[:, cols] = (blk * lax.rsqrt(ms + EPS) * g).astype(o_ref.dtype)

    @pl.when(j >= n_norm_tiles)
    def _():
        o_ref[...] = acc.astype(o_ref.dtype)


def in_proj(h, w, gains, *, col0, n_cols, n_norm_cols, out_dtype, tm, tn):
    t, k = h.shape
    n_sec = gains.shape[0]
    sec_tiles = max(n_norm_cols // tn // n_sec, 1)
    jb0 = col0 // tn
    return pl.pallas_call(
        functools.partial(_in_proj_kernel, n_norm_tiles=n_norm_cols // tn),
        grid=(t // tm, n_cols // tn),
        in_specs=[pl.BlockSpec((tm, k), lambda i, j: (i, 0)),
                  pl.BlockSpec((k, tn), lambda i, j: (0, j + jb0)),
                  pl.BlockSpec((1, 1, HEAD_DIM),
                               lambda i, j: (jnp.minimum(j // sec_tiles, n_sec - 1), 0, 0))],
        out_specs=pl.BlockSpec((tm, tn), lambda i, j: (i, j)),
        out_shape=jax.ShapeDtypeStruct((t, n_cols), out_dtype),
        compiler_params=_params("parallel", "arbitrary"),
        name="in_proj",
    )(h, w, gains.reshape(n_sec, 1, HEAD_DIM).astype(F32))


def _rel_bias_diagonals(rel_bias, tq):
    w = LEFT + tq
    length = pl.cdiv(w + tq - 1, LANES) * LANES
    m = jnp.arange(length)
    d = jnp.where(m < w, m, m - length)
    rel = jnp.clip(LEFT - d, -MAX_REL, MAX_REL) + MAX_REL
    return rel_bias.astype(F32)[:, None, rel]


def _attn_kernel(*refs, n_prev, blocks_per_seq, scale):
    q_ref = refs[0]
    k_refs = refs[1:2 + n_prev]
    v_refs = refs[2 + n_prev:3 + 2 * n_prev]
    diag_ref, g_ref, o_ref, tbl_scr, o_scr = refs[3 + 2 * n_prev:]
    heads, tq, w = tbl_scr.shape
    i = pl.program_id(0)

    @pl.when(i == 0)
    def _():
        r = lax.broadcasted_iota(jnp.int32, (tq, w), 0)
        c = lax.broadcasted_iota(jnp.int32, (tq, w), 1)
        dchunk = r // CHUNK - c // CHUNK + LEFT_CHUNKS
        band = (dchunk >= 0) & (dchunk <= LEFT_CHUNKS)
        for hh in range(heads):
            diag = jnp.broadcast_to(diag_ref[hh], (tq, diag_ref.shape[2]))
            bias = pltpu.roll(diag, 0, axis=1, stride=1, stride_axis=0)[:, :w]
            tbl_scr[hh] = jnp.where(band, bias * LOG2_E, NEG)

    def heads_loop(mask_fn):
        for hh in range(heads):
            cols = slice(hh * HEAD_DIM, (hh + 1) * HEAD_DIM)
            k = jnp.concatenate([r[:, cols] for r in k_refs], axis=0)
            v = jnp.concatenate([r[:, cols] for r in v_refs], axis=0)
            s = lax.dot_general(q_ref[:, cols], k, _NT, preferred_element_type=F32)
            s = mask_fn(s * (scale * LOG2_E) + tbl_scr[hh])
            m = jnp.max(s, axis=-1, keepdims=True)
            p = jnp.exp2(s - m)
            l = jnp.sum(p, axis=-1, keepdims=True)
            o = jnp.dot(p.astype(v.dtype), v, preferred_element_type=F32)
            o_scr[:, cols] = o / l

    jb = i % blocks_per_seq

    @pl.when(jb < n_prev)
    def _():
        col = lax.broadcasted_iota(jnp.int32, (tq, w), 1)
        in_seq = col >= (n_prev - jb) * tq
        heads_loop(lambda s: jnp.where(in_seq, s, NEG))

    @pl.when(jb >= n_prev)
    def _():
        heads_loop(lambda s: s)

    attn = o_scr[...]
    ms = jnp.mean(attn * attn, axis=-1, keepdims=True)
    o_ref[...] = (attn * lax.rsqrt(ms + EPS) * g_ref[...]).astype(o_ref.dtype)


def band_attention(qkv, rel_bias, g_out, *, seq, n_heads, tq):
    t = qkv.shape[0]
    n_prev = LEFT // tq
    bps = seq // tq
    d_attn = n_heads * HEAD_DIM
    diags = _rel_bias_diagonals(rel_bias, tq)

    def win_spec(col_block, back):
        def idx(i):
            first = (i // bps) * bps
            return (jnp.maximum(i - back, first), col_block)
        return pl.BlockSpec((tq, d_attn), idx)

    backs = list(range(n_prev, -1, -1))
    in_specs = ([pl.BlockSpec((tq, d_attn), lambda i: (i, 0))]
                + [win_spec(1, b) for b in backs]
                + [win_spec(2, b) for b in backs]
                + [pl.BlockSpec(diags.shape, lambda i: (0, 0, 0)),
                   pl.BlockSpec((1, d_attn), lambda i: (0, 0))])
    n_win = n_prev + 1
    return pl.pallas_call(
        functools.partial(_attn_kernel, n_prev=n_prev, blocks_per_seq=bps,
                          scale=HEAD_DIM ** -0.5),
        grid=(t // tq,),
        in_specs=in_specs,
        out_specs=pl.BlockSpec((tq, d_attn), lambda i: (i, 0)),
        out_shape=jax.ShapeDtypeStruct((t, d_attn), BF16),
        scratch_shapes=[pltpu.VMEM((n_heads, tq, LEFT + tq), F32),
                        pltpu.VMEM((tq, d_attn), F32)],
        compiler_params=_params("arbitrary"),
        name="band_attention",
    )(*([qkv] * (1 + 2 * n_win)), diags, g_out.reshape(1, d_attn).astype(F32))


def _rec_kernel(x_ref, y_ref, cw_ref, cb_ref, wa_ref, wx_ref, ba_ref, bx_ref, lam_ref, g_ref,
                o_ref, xpad, hcar, rec_scr, *, lane_chunk):
    tt, c_all = x_ref.shape
    groups = tt // SUBLANES
    pad0 = SUBLANES - (CONV_WIDTH - 1)

    @pl.when(pl.program_id(1) == 0)
    def _():
        xpad[0:SUBLANES, :] = jnp.zeros((SUBLANES, c_all), F32)
        hcar[...] = jnp.zeros_like(hcar)

    xpad[SUBLANES:SUBLANES + tt, :] = x_ref[...]
    ss = jnp.zeros((tt, 1), F32)
    row = lax.broadcasted_iota(jnp.int32, (groups, SUBLANES, lane_chunk), 1)
    for cbk in range(c_all // lane_chunk):
        ls = slice(cbk * lane_chunk, (cbk + 1) * lane_chunk)
        xr = cb_ref[:, ls]
        for tap in range(CONV_WIDTH):
            xr = xr + xpad[pad0 + tap:pad0 + tap + tt, ls] * cw_ref[tap:tap + 1, ls]
        xr_b = xr.astype(BF16)
        pre_a, pre_x = [], []
        for n in range(lane_chunk // REC_BLOCK_DIM):
            nb = cbk * (lane_chunk // REC_BLOCK_DIM) + n
            xs = xr_b[:, n * REC_BLOCK_DIM:(n + 1) * REC_BLOCK_DIM]
            pre_a.append(jnp.dot(xs, wa_ref[nb], preferred_element_type=F32))
            pre_x.append(jnp.dot(xs, wx_ref[nb], preferred_element_type=F32))
        r = _sigmoid(jnp.concatenate(pre_a, axis=1) + ba_ref[:, ls])
        gate_i = _sigmoid(jnp.concatenate(pre_x, axis=1) + bx_ref[:, ls])
        z = -lam_ref[:, ls]
        softplus = jnp.maximum(z, 0.0) + jnp.log1p(jnp.exp(-jnp.abs(z)))
        log_a = (-RG_C * r) * softplus
        a = jnp.exp(log_a)
        m2 = -jnp.tanh(log_a) * (a * a + 1.0)
        mult = jnp.where(m2 > 0.0, m2 * lax.rsqrt(m2), 0.0)
        u = mult * (gate_i * xr)
        a3 = a.reshape(groups, SUBLANES, lane_chunk)
        u3 = u.reshape(groups, SUBLANES, lane_chunk)
        d = 1
        while d < SUBLANES:
            keep = row >= d
            a_sh = jnp.where(keep, pltpu.roll(a3, d, axis=1), 1.0)
            u_sh = jnp.where(keep, pltpu.roll(u3, d, axis=1), 0.0)
            u3 = a3 * u_sh + u3
            a3 = a3 * a_sh
            d *= 2
        h = hcar[:, ls]
        hs = []
        for gi in range(groups):
            hg = u3[gi] + a3[gi] * h
            hs.append(hg)
            h = hg[SUBLANES - 1:SUBLANES, :]
        hcar[:, ls] = h
        rec = jnp.concatenate(hs, axis=0) * _gelu_tanh(y_ref[:, ls])
        rec_scr[:, ls] = rec
        ss = ss + jnp.sum(rec * rec, axis=-1, keepdims=True)
    xpad[0:SUBLANES, :] = x_ref[tt - SUBLANES:tt, :]
    o_ref[...] = (rec_scr[...] * lax.rsqrt(ss / c_all + EPS) * g_ref[...]).astype(o_ref.dtype)


def rec_branch(xy, conv_w, conv_b, w_a, b_a, w_x, b_x, lam, g_out, *, batch, tt, lane_chunk):
    t, c2 = xy.shape
    c = c2 // 2
    nt = t // batch // tt
    row = lambda v: v.reshape(1, c).astype(F32)
    vec = pl.BlockSpec((1, c), lambda b, i: (0, 0))
    wspec = pl.BlockSpec(w_a.shape, lambda b, i: (0, 0, 0))
    return pl.pallas_call(
        functools.partial(_rec_kernel, lane_chunk=lane_chunk),
        grid=(batch, nt),
        in_specs=[pl.BlockSpec((tt, c), lambda b, i: (b * nt + i, 0)),
                  pl.BlockSpec((tt, c), lambda b, i: (b * nt + i, 1)),
                  pl.BlockSpec((CONV_WIDTH, c), lambda b, i: (0, 0)),
                  vec, wspec, wspec, vec, vec, vec, vec],
        out_specs=pl.BlockSpec((tt, c), lambda b, i: (b * nt + i, 0)),
        out_shape=jax.ShapeDtypeStruct((t, c), BF16),
        scratch_shapes=[pltpu.VMEM((SUBLANES + tt, c), F32),
                        pltpu.VMEM((1, c), F32),
                        pltpu.VMEM((tt, c), F32)],
        compiler_params=_params("parallel", "arbitrary"),
        name="rec_branch",
    )(xy, xy, conv_w.astype(F32), row(conv_b), w_a.astype(BF16), w_x.astype(BF16),
      row(b_a), row(b_x), row(lam), row(g_out))


def _out_proj_kernel(a1_ref, a2_ref, b1_ref, b2_ref, x_ref, o_ref):
    acc = jnp.dot(a1_ref[...], b1_ref[...], preferred_element_type=F32)
    acc = acc + jnp.dot(a2_ref[...], b2_ref[...], preferred_element_type=F32)
    o_ref[...] = x_ref[...] + acc


def out_proj(a1, a2, w, x, *, tm, tn):
    t, k1 = a1.shape
    n = w.shape[1]
    return pl.pallas_call(
        _out_proj_kernel,
        grid=(t // tm, n // tn),
        in_specs=[pl.BlockSpec((tm, k1), lambda i, j: (i, 0)),
                  pl.BlockSpec((tm, k1), lambda i, j: (i, 0)),
                  pl.BlockSpec((k1, tn), lambda i, j: (0, j)),
                  pl.BlockSpec((k1, tn), lambda i, j: (1, j)),
                  pl.BlockSpec((tm, tn), lambda i, j: (i, j))],
        out_specs=pl.BlockSpec((tm, tn), lambda i, j: (i, j)),
        out_shape=jax.ShapeDtypeStruct((t, n), F32),
        compiler_params=_params("parallel", "arbitrary"),
        name="out_proj",
    )(a1, a2, w, w, x)


def _peer_scores_kernel(a_ref, b_ref, keys_ref, st_ref):
    q = jnp.dot(a_ref[...], b_ref[...], preferred_element_type=F32).astype(BF16)
    for c in range(keys_ref.shape[0]):
        rows = slice(c * PEER_N_KEYS, (c + 1) * PEER_N_KEYS)
        st_ref[rows, :] = lax.dot_general(keys_ref[c], q[:, rows], _NT,
                                          preferred_element_type=F32)


def peer_scores(h, w_q, keys, *, tm, tn):
    t, k = h.shape
    n = w_q.shape[1]
    kb = tn // PEER_N_KEYS
    return pl.pallas_call(
        _peer_scores_kernel,
        grid=(t // tm, n // tn),
        in_specs=[pl.BlockSpec((tm, k), lambda i, j: (i, 0)),
                  pl.BlockSpec((k, tn), lambda i, j: (0, j)),
                  pl.BlockSpec((kb,) + keys.shape[1:], lambda i, j: (j, 0, 0))],
        out_specs=pl.BlockSpec((tn, tm), lambda i, j: (j, i)),
        out_shape=jax.ShapeDtypeStruct((n, t), F32),
        compiler_params=_params("parallel", "arbitrary"),
        name="peer_scores",
    )(h, w_q, keys)


_CAND_PAIRS = [(a, b) for a in range(PEER_TOPK + 1) for b in range(PEER_TOPK + 1)
               if (a + 1) * (b + 1) <= PEER_TOPK + 1]


def _sorting_network(n):
    pairs = []
    p = 1
    while p < n:
        k = p
        while k >= 1:
            for j in range(k % p, n - k, 2 * k):
                for i in range(min(k, n - j - k)):
                    if (i + j) // (2 * p) == (i + j + k) // (2 * p):
                        pairs.append((i + j, i + j + k))
            k //= 2
        p *= 2
    return pairs


def _top_values(s, count):
    groups = s.shape[0] // SUBLANES
    v = [s[g * SUBLANES:(g + 1) * SUBLANES] for g in range(groups)]
    for a, b in _sorting_network(groups):
        v[a], v[b] = jnp.maximum(v[a], v[b]), jnp.minimum(v[a], v[b])
    rows = []
    for k in range(count):
        m = jnp.max(v[0], axis=0, keepdims=True)
        rows.append(m)
        live = min(groups, count - 1 - k)
        if live:
            held = v[0] == m
            for g in range(live):
                v[g] = jnp.where(held, v[g + 1] if g + 1 < groups else NEG, v[g])
    return rows


def _peer_topk_kernel(s_ref, e2_ref, fac_ref, thr_scr, hc_scr):
    nh, _, nk, _ = s_ref.shape

    def head(h, carry):
        s1 = s_ref[h, 0]
        s2 = s_ref[h, 1]
        t1 = _top_values(s1, PEER_TOPK + 1)
        t2 = _top_values(s2, PEER_TOPK + 1)
        pad_rows = pl.next_power_of_2(pl.cdiv(len(_CAND_PAIRS), SUBLANES)) * SUBLANES
        cands = jnp.concatenate(
            [t1[a] + t2[b] for a, b in _CAND_PAIRS]
            + [jnp.full((pad_rows - len(_CAND_PAIRS), s1.shape[1]), NEG, F32)], axis=0)
        top = _top_values(cands, PEER_TOPK + 1)
        tau = 0.5 * (top[PEER_TOPK - 1] + top[PEER_TOPK])
        z = jnp.sum(jnp.where(cands > tau, jnp.exp(cands - top[0]), 0.0), axis=0, keepdims=True)
        thr_scr[h] = tau - s1
        hc_scr[h] = (0.5 * jnp.exp(s1 - t1[0])) / z
        e2_ref[h] = pltpu.bitcast(jnp.exp(s2 - t2[0]).astype(BF16), jnp.uint32)
        return carry

    lax.fori_loop(0, nh, head, 0)
    for h in range(nh):
        for lc in range(fac_ref.shape[0]):
            ls = slice(lc * LANES, (lc + 1) * LANES)
            fac_ref[lc, 0, pl.ds(h, nk, stride=nh), :] = thr_scr[h, :, ls]
            fac_ref[lc, 1, pl.ds(h, nk, stride=nh), :] = hc_scr[h, :, ls]


def peer_topk(st, *, tt):
    nh, _, nk, t = st.shape
    key_major = pl.BlockSpec((tt // LANES, 2, nk * nh, LANES), lambda i: (i, 0, 0, 0))
    key_major_shape = jax.ShapeDtypeStruct((t // LANES, 2, nk * nh, LANES), F32)
    return pl.pallas_call(
        _peer_topk_kernel,
        grid=(t // tt,),
        in_specs=[pl.BlockSpec((nh, 2, nk, tt), lambda i: (0, 0, 0, i))],
        out_specs=[pl.BlockSpec((nh, nk // 2, tt), lambda i: (0, 0, i)), key_major],
        out_shape=[jax.ShapeDtypeStruct((nh, nk // 2, t), jnp.uint32), key_major_shape],
        scratch_shapes=[pltpu.VMEM((nh, nk, tt), F32), pltpu.VMEM((nh, nk, tt), F32)],
        compiler_params=_params("parallel"),
        name="peer_topk",
    )(st)


def _peer_dense_kernel(h_ref, uv_ref, s2_ref, e2_ref, fac_ref, x_ref, o_ref, act_scr, w_scr):
    te, tt = act_scr.shape
    nh, nk = s2_ref.shape[0], s2_ref.shape[1]

    @pl.when(pl.program_id(1) == 0)
    def _():
        o_ref[...] = x_ref[...]

    act_scr[...] = lax.dot_general(uv_ref[0], h_ref[...], _NT, preferred_element_type=F32)
    for ii in range(te // nk):
        for lc in range(tt // LANES):
            ls = slice(lc * LANES, (lc + 1) * LANES)
            half_gate = None
            for h in range(nh):
                r = ii * nh + h
                half_c = jnp.broadcast_to(fac_ref[lc, 1, r:r + 1, :], (nk, LANES)).astype(BF16)
                weight = pltpu.bitcast(e2_ref[h, :, ls], BF16) * half_c
                part = jnp.where(s2_ref[h, :, ls] >= fac_ref[lc, 0, r:r + 1, :], weight,
                                 jnp.zeros_like(weight))
                half_gate = part if half_gate is None else half_gate + part
            a = act_scr[ii * nk:(ii + 1) * nk, ls]
            t = jnp.tanh(a * (_GELU_C0 + _GELU_C1 * (a * a)))
            w_tile = (half_gate * a) * (1.0 + t)
            w_scr[ls, ii * nk:(ii + 1) * nk] = w_tile.T.astype(w_scr.dtype)
    o_ref[...] += jnp.dot(w_scr[...], uv_ref[1], preferred_element_type=F32)


def peer_dense(h, uv, st, e2, fac, x, *, tt, te):
    t, d = h.shape
    e = uv.shape[1]
    nh, _, nk, _ = st.shape
    once = pl.Buffered(1)
    key_rows = te // nk * nh
    return pl.pallas_call(
        _peer_dense_kernel,
        grid=(t // tt, e // te),
        in_specs=[pl.BlockSpec((tt, d), lambda i, j: (i, 0), pipeline_mode=once),
                  pl.BlockSpec((2, te, d), lambda i, j: (0, j, 0)),
                  pl.BlockSpec((nh, None, nk, tt), lambda i, j: (0, 1, 0, i), pipeline_mode=once),
                  pl.BlockSpec((nh, nk // 2, tt), lambda i, j: (0, 0, i), pipeline_mode=once),
                  pl.BlockSpec((tt // LANES, 2, key_rows, LANES), lambda i, j: (i, 0, j, 0)),
                  pl.BlockSpec((tt, d), lambda i, j: (i, 0))],
        out_specs=pl.BlockSpec((tt, d), lambda i, j: (i, 0)),
        out_shape=jax.ShapeDtypeStruct((t, d), F32),
        scratch_shapes=[pltpu.VMEM((te, tt), F32), pltpu.VMEM((tt, te), BF16)],
        compiler_params=_params("parallel", "arbitrary"),
        name="peer_dense",
    )(h, uv, st, e2, fac, x)


def hybrid_layer(x, g_norm1, w_in, g_q_norm, g_k_norm, rel_bias, conv_w, conv_b,
                 w_rg_a, b_rg_a, w_rg_x, b_rg_x, rg_lambda, g_attn_out, g_rec_out,
                 w_out, g_norm2, w_peer_q, peer_keys_1, peer_keys_2, peer_u, peer_v,
                 *, cfg):
    b, s, d = x.shape
    t = b * s
    d_attn = g_attn_out.shape[0]
    d_rec = g_rec_out.shape[0]
    n_heads = d_attn // HEAD_DIM
    x2 = x.reshape(t, d)

    h1 = rmsnorm_rows(x2, g_norm1, cfg["norm_tm"])
    w_in_b = w_in.astype(BF16)
    qkv = in_proj(h1, w_in_b, jnp.stack([g_q_norm, g_k_norm]), col0=0, n_cols=3 * d_attn,
                  n_norm_cols=2 * d_attn, out_dtype=BF16, tm=cfg["mm_tm"], tn=cfg["mm_tn"])
    xy = in_proj(h1, w_in_b, jnp.ones((1, HEAD_DIM), F32), col0=3 * d_attn, n_cols=2 * d_rec,
                 n_norm_cols=0, out_dtype=F32, tm=cfg["mm_tm"], tn=cfg["mm_tn"])

    attn_n = band_attention(qkv, rel_bias, g_attn_out, seq=s, n_heads=n_heads,
                            tq=cfg["attn_tq"])
    rec_n = rec_branch(xy, conv_w, conv_b, w_rg_a, b_rg_a, w_rg_x, b_rg_x, rg_lambda, g_rec_out,
                       batch=b, tt=cfg["rec_tt"], lane_chunk=cfg["rec_lanes"])
    x1 = out_proj(attn_n, rec_n, w_out.astype(BF16), x2, tm=cfg["mm_tm"], tn=cfg["mm_tn"])

    h2 = rmsnorm_rows(x1, g_norm2, cfg["norm_tm"])
    nh, nk, half = peer_keys_1.shape
    keys = jnp.stack([peer_keys_1, peer_keys_2], axis=1).reshape(2 * nh, nk, half).astype(BF16)
    st = peer_scores(h2, w_peer_q.astype(BF16), keys, tm=cfg["mm_tm"], tn=cfg["mm_tn"])
    st = st.reshape(nh, 2, nk, t)
    e2, fac = peer_topk(st, tt=cfg["topk_tt"])
    uv = jnp.stack([peer_u.astype(BF16), peer_v.astype(BF16)])
    out = peer_dense(h2, uv, st, e2, fac, x1,
                     tt=cfg["peer_tt"], te=cfg["peer_te"])
    return out.reshape(b, s, d)


_CFG = dict(norm_tm=256, mm_tm=1024, mm_tn=1024, attn_tq=256, rec_tt=256, rec_lanes=512,
            topk_tt=256, peer_tt=512, peer_te=512)


def kernel(x, g_norm1, w_in, g_q_norm, g_k_norm, rel_bias, conv_w, conv_b, w_rg_a, b_rg_a,
           w_rg_x, b_rg_x, rg_lambda, g_attn_out, g_rec_out, w_out, g_norm2, w_peer_q,
           peer_keys_1, peer_keys_2, peer_u, peer_v):
    for l in range(g_norm1.shape[0]):
        x = hybrid_layer(x, g_norm1[l], w_in[l], g_q_norm[l], g_k_norm[l], rel_bias[l],
                         conv_w[l], conv_b[l], w_rg_a[l], b_rg_a[l], w_rg_x[l], b_rg_x[l],
                         rg_lambda[l], g_attn_out[l], g_rec_out[l], w_out[l], g_norm2[l],
                         w_peer_q[l], peer_keys_1[l], peer_keys_2[l], peer_u[l], peer_v[l],
                         cfg=_CFG)
    return x
```

```python
import functools
import math

import jax
import jax.numpy as jnp
from jax import lax
from jax.experimental import pallas as pl
from jax.experimental.pallas import tpu as pltpu

EPS = 1e-6
CHUNK = 64
LEFT_CHUNKS = 8
LEFT = LEFT_CHUNKS * CHUNK
MAX_REL = 256
HEAD_DIM = 128
REC_BLOCK_DIM = 128
CONV_WIDTH = 4
RG_C = 8.0
PEER_HEADS = 8
PEER_N_KEYS = 128
PEER_TOPK = 16
LANES = 128
SUBLANES = 8
NEG = -1e30
LOG2_E = math.log2(math.e)
VMEM_LIMIT_BYTES = 60000 * 1024

F32 = jnp.float32
BF16 = jnp.bfloat16
_NT = (((1,), (1,)), ((), ()))


def _params(*sem):
    return pltpu.CompilerParams(dimension_semantics=sem, vmem_limit_bytes=VMEM_LIMIT_BYTES)


_GELU_C0 = math.sqrt(2.0 / math.pi)
_GELU_C1 = 0.044715 * _GELU_C0


def _gelu_tanh(x):
    return (0.5 * x) * (1.0 + jnp.tanh(x * (_GELU_C0 + _GELU_C1 * (x * x))))


def _sigmoid(x):
    return 0.5 + 0.5 * jnp.tanh(0.5 * x)


def _rmsnorm_kernel(x_ref, g_ref, o_ref):
    x = x_ref[...].astype(F32)
    ms = jnp.mean(x * x, axis=-1, keepdims=True)
    o_ref[...] = (x * lax.rsqrt(ms + EPS) * g_ref[...]).astype(o_ref.dtype)


def rmsnorm_rows(x, g, tm):
    t, d = x.shape
    return pl.pallas_call(
        _rmsnorm_kernel,
        grid=(t // tm,),
        in_specs=[pl.BlockSpec((tm, d), lambda i: (i, 0)),
                  pl.BlockSpec((1, d), lambda i: (0, 0))],
        out_specs=pl.BlockSpec((tm, d), lambda i: (i, 0)),
        out_shape=jax.ShapeDtypeStruct((t, d), BF16),
        compiler_params=_params("parallel"),
        name="rmsnorm_rows",
    )(x, g.reshape(1, d).astype(F32))


def _in_proj_kernel(a_ref, b_ref, g_ref, o_ref, *, n_norm_tiles):
    acc = jnp.dot(a_ref[...], b_ref[...], preferred_element_type=F32)
    j = pl.program_id(1)
    tn = o_ref.shape[1]

    @pl.when(j < n_norm_tiles)
    def _():
        g = g_ref[0]
        for c in range(tn // HEAD_DIM):
            cols = slice(c * HEAD_DIM, (c + 1) * HEAD_DIM)
            blk = acc[:, cols]
            ms = jnp.mean(blk * blk, axis=-1, keepdims=True)
            o_ref[:, cols] = (blk * lax.rsqrt(ms + EPS) * g).astype(o_ref.dtype)

    @pl.when(j >= n_norm_tiles)
    def _():
        o_ref[...] = acc.astype(o_ref.dtype)


def in_proj(h, w, gains, *, col0, n_cols, n_norm_cols, out_dtype, tm, tn):
    t, k = h.shape
    n_sec = gains.shape[0]
    sec_tiles = max(n_norm_cols // tn // n_sec, 1)
    jb0 = col0 // tn
    return pl.pallas_call(
        functools.partial(_in_proj_kernel, n_norm_tiles=n_norm_cols // tn),
        grid=(t // tm, n_cols // tn),
        in_specs=[pl.BlockSpec((tm, k), lambda i, j: (i, 0)),
                  pl.BlockSpec((k, tn), lambda i, j: (0, j + jb0)),
                  pl.BlockSpec((1, 1, HEAD_DIM),
                               lambda i, j: (jnp.minimum(j // sec_tiles, n_sec - 1), 0, 0))],
        out_specs=pl.BlockSpec((tm, tn), lambda i, j: (i, j)),
        out_shape=jax.ShapeDtypeStruct((t, n_cols), out_dtype),
        compiler_params=_params("parallel", "arbitrary"),
        name="in_proj",
    )(h, w, gains.reshape(n_sec, 1, HEAD_DIM).astype(F32))


def _rel_bias_diagonals(rel_bias, tq):
    w = LEFT + tq
    length = pl.cdiv(w + tq - 1, LANES) * LANES
    m = jnp.arange(length)
    d = jnp.where(m < w, m, m - length)
    rel = jnp.clip(LEFT - d, -MAX_REL, MAX_REL) + MAX_REL
    return rel_bias.astype(F32)[:, None, rel]


def _attn_kernel(*refs, n_prev, blocks_per_seq, scale):
    q_ref = refs[0]
    k_refs = refs[1:2 + n_prev]
    v_refs = refs[2 + n_prev:3 + 2 * n_prev]
    diag_ref, g_ref, o_ref, tbl_scr, o_scr = refs[3 + 2 * n_prev:]
    heads, tq, w = tbl_scr.shape
    i = pl.program_id(0)

    @pl.when(i == 0)
    def _():
        r = lax.broadcasted_iota(jnp.int32, (tq, w), 0)
        c = lax.broadcasted_iota(jnp.int32, (tq, w), 1)
        dchunk = r // CHUNK - c // CHUNK + LEFT_CHUNKS
        band = (dchunk >= 0) & (dchunk <= LEFT_CHUNKS)
        for hh in range(heads):
            diag = jnp.broadcast_to(diag_ref[hh], (tq, diag_ref.shape[2]))
            bias = pltpu.roll(diag, 0, axis=1, stride=1, stride_axis=0)[:, :w]
            tbl_scr[hh] = jnp.where(band, bias * LOG2_E, NEG)

    def heads_loop(mask_fn):
        for hh in range(heads):
            cols = slice(hh * HEAD_DIM, (hh + 1) * HEAD_DIM)
            k = jnp.concatenate([r[:, cols] for r in k_refs], axis=0)
            v = jnp.concatenate([r[:, cols] for r in v_refs], axis=0)
            s = lax.dot_general(q_ref[:, cols], k, _NT, preferred_element_type=F32)
            s = mask_fn(s * (scale * LOG2_E) + tbl_scr[hh])
            m = jnp.max(s, axis=-1, keepdims=True)
            p = jnp.exp2(s - m)
            l = jnp.sum(p, axis=-1, keepdims=True)
            o = jnp.dot(p.astype(v.dtype), v, preferred_element_type=F32)
            o_scr[:, cols] = o / l

    jb = i % blocks_per_seq

    @pl.when(jb < n_prev)
    def _():
        col = lax.broadcasted_iota(jnp.int32, (tq, w), 1)
        in_seq = col >= (n_prev - jb) * tq
        heads_loop(lambda s: jnp.where(in_seq, s, NEG))

    @pl.when(jb >= n_prev)
    def _():
        heads_loop(lambda s: s)

    attn = o_scr[...]
    ms = jnp.mean(attn * attn, axis=-1, keepdims=True)
    o_ref[...] = (attn * lax.rsqrt(ms + EPS) * g_ref[...]).astype(o_ref.dtype)


def band_attention(qkv, rel_bias, g_out, *, seq, n_heads, tq):
    t = qkv.shape[0]
    n_prev = LEFT // tq
    bps = seq // tq
    d_attn = n_heads * HEAD_DIM
    diags = _rel_bias_diagonals(rel_bias, tq)

    def win_spec(col_block, back):
        def idx(i):
            first = (i // bps) * bps
            return (jnp.maximum(i - back, first), col_block)
        return pl.BlockSpec((tq, d_attn), idx)

    backs = list(range(n_prev, -1, -1))
    in_specs = ([pl.BlockSpec((tq, d_attn), lambda i: (i, 0))]
                + [win_spec(1, b) for b in backs]
                + [win_spec(2, b) for b in backs]
                + [pl.BlockSpec(diags.shape, lambda i: (0, 0, 0)),
                   pl.BlockSpec((1, d_attn), lambda i: (0, 0))])
    n_win = n_prev + 1
    return pl.pallas_call(
        functools.partial(_attn_kernel, n_prev=n_prev, blocks_per_seq=bps,
                          scale=HEAD_DIM ** -0.5),
        grid=(t // tq,),
        in_specs=in_specs,
        out_specs=pl.BlockSpec((tq, d_attn), lambda i: (i, 0)),
        out_shape=jax.ShapeDtypeStruct((t, d_attn), BF16),
        scratch_shapes=[pltpu.VMEM((n_heads, tq, LEFT + tq), F32),
                        pltpu.VMEM((tq, d_attn), F32)],
        compiler_params=_params("arbitrary"),
        name="band_attention",
    )(*([qkv] * (1 + 2 * n_win)), diags, g_out.reshape(1, d_attn).astype(F32))


def _rec_kernel(x_ref, y_ref, cw_ref, cb_ref, wa_ref, wx_ref, ba_ref, bx_ref, lam_ref, g_ref,
                side_ref, o_ref, side_out_ref, xpad, hcar, rec_scr, *, lane_chunk):
    side_out_ref[...] = side_ref[...].astype(side_out_ref.dtype)
    tt, c_all = x_ref.shape
    groups = tt // SUBLANES
    pad0 = SUBLANES - (CONV_WIDTH - 1)

    @pl.when(pl.program_id(1) == 0)
    def _():
        xpad[0:SUBLANES, :] = jnp.zeros((SUBLANES, c_all), F32)
        hcar[...] = jnp.zeros_like(hcar)

    xpad[SUBLANES:SUBLANES + tt, :] = x_ref[...]
    ss = jnp.zeros((tt, 1), F32)
    row = lax.broadcasted_iota(jnp.int32, (groups, SUBLANES, lane_chunk), 1)
    for cbk in range(c_all // lane_chunk):
        ls = slice(cbk * lane_chunk, (cbk + 1) * lane_chunk)
        xr = cb_ref[:, ls]
        for tap in range(CONV_WIDTH):
            xr = xr + xpad[pad0 + tap:pad0 + tap + tt, ls] * cw_ref[tap:tap + 1, ls]
        xr_b = xr.astype(BF16)
        pre_a, pre_x = [], []
        for n in range(lane_chunk // REC_BLOCK_DIM):
            nb = cbk * (lane_chunk // REC_BLOCK_DIM) + n
            xs = xr_b[:, n * REC_BLOCK_DIM:(n + 1) * REC_BLOCK_DIM]
            pre_a.append(jnp.dot(xs, wa_ref[nb], preferred_element_type=F32))
            pre_x.append(jnp.dot(xs, wx_ref[nb], preferred_element_type=F32))
        r = _sigmoid(jnp.concatenate(pre_a, axis=1) + ba_ref[:, ls])
        gate_i = _sigmoid(jnp.concatenate(pre_x, axis=1) + bx_ref[:, ls])
        z = -lam_ref[:, ls]
        softplus = jnp.maximum(z, 0.0) + jnp.log1p(jnp.exp(-jnp.abs(z)))
        log_a = (-RG_C * r) * softplus
        a = jnp.exp(log_a)
        m2 = -jnp.tanh(log_a) * (a * a + 1.0)
        mult = jnp.where(m2 > 0.0, m2 * lax.rsqrt(m2), 0.0)
        u = mult * (gate_i * xr)
        a3 = a.reshape(groups, SUBLANES, lane_chunk)
        u3 = u.reshape(groups, SUBLANES, lane_chunk)
        d = 1
        while d < SUBLANES:
            keep = row >= d
            a_sh = jnp.where(keep, pltpu.roll(a3, d, axis=1), 1.0)
            u_sh = jnp.where(keep, pltpu.roll(u3, d, axis=1), 0.0)
            u3 = a3 * u_sh + u3
            a3 = a3 * a_sh
            d *= 2
        h = hcar[:, ls]
        hs = []
        for gi in range(groups):
            hg = u3[gi] + a3[gi] * h
            hs.append(hg)
            h = hg[SUBLANES - 1:SUBLANES, :]
        hcar[:, ls] = h
        rec = jnp.concatenate(hs, axis=0) * _gelu_tanh(y_ref[:, ls])
        rec_scr[:, ls] = rec
        ss = ss + jnp.sum(rec * rec, axis=-1, keepdims=True)
    xpad[0:SUBLANES, :] = x_ref[tt - SUBLANES:tt, :]
    o_ref[...] = (rec_scr[...] * lax.rsqrt(ss / c_all + EPS) * g_ref[...]).astype(o_ref.dtype)


def rec_branch(xy, conv_w, conv_b, w_a, b_a, w_x, b_x, lam, g_out, side, *, batch, tt,
               lane_chunk):
    t, c2 = xy.shape
    c = c2 // 2
    nt = t // batch // tt
    assert side.shape[0] % (batch * nt) == 0
    side_spec = pl.BlockSpec((side.shape[0] // (batch * nt), side.shape[1]),
                             lambda b, i: (b * nt + i, 0))
    row = lambda v: v.reshape(1, c).astype(F32)
    vec = pl.BlockSpec((1, c), lambda b, i: (0, 0))
    wspec = pl.BlockSpec(w_a.shape, lambda b, i: (0, 0, 0))
    return pl.pallas_call(
        functools.partial(_rec_kernel, lane_chunk=lane_chunk),
        grid=(batch, nt),
        in_specs=[pl.BlockSpec((tt, c), lambda b, i: (b * nt + i, 0)),
                  pl.BlockSpec((tt, c), lambda b, i: (b * nt + i, 1)),
                  pl.BlockSpec((CONV_WIDTH, c), lambda b, i: (0, 0)),
                  vec, wspec, wspec, vec, vec, vec, vec, side_spec],
        out_specs=[pl.BlockSpec((tt, c), lambda b, i: (b * nt + i, 0)), side_spec],
        out_shape=[jax.ShapeDtypeStruct((t, c), BF16), jax.ShapeDtypeStruct(side.shape, BF16)],
        scratch_shapes=[pltpu.VMEM((SUBLANES + tt, c), F32),
                        pltpu.VMEM((1, c), F32),
                        pltpu.VMEM((tt, c), F32)],
        compiler_params=_params("parallel", "arbitrary"),
        name="rec_branch",
    )(xy, xy, conv_w.astype(F32), row(conv_b), w_a.astype(BF16), w_x.astype(BF16),
      row(b_a), row(b_x), row(lam), row(g_out), side)


def _out_proj_kernel(a1_ref, a2_ref, b1_ref, b2_ref, x_ref, o_ref):
    acc = jnp.dot(a1_ref[...], b1_ref[...], preferred_element_type=F32)
    acc = acc + jnp.dot(a2_ref[...], b2_ref[...], preferred_element_type=F32)
    o_ref[...] = x_ref[...] + acc


def out_proj(a1, a2, w, x, *, tm, tn):
    t, k1 = a1.shape
    n = w.shape[1]
    return pl.pallas_call(
        _out_proj_kernel,
        grid=(t // tm, n // tn),
        in_specs=[pl.BlockSpec((tm, k1), lambda i, j: (i, 0)),
                  pl.BlockSpec((tm, k1), lambda i, j: (i, 0)),
                  pl.BlockSpec((k1, tn), lambda i, j: (0, j)),
                  pl.BlockSpec((k1, tn), lambda i, j: (1, j)),
                  pl.BlockSpec((tm, tn), lambda i, j: (i, j))],
        out_specs=pl.BlockSpec((tm, tn), lambda i, j: (i, j)),
        out_shape=jax.ShapeDtypeStruct((t, n), F32),
        compiler_params=_params("parallel", "arbitrary"),
        name="out_proj",
    )(a1, a2, w, w, x)


def _peer_scores_kernel(a_ref, b_ref, keys_ref, st_ref):
    q = jnp.dot(a_ref[...], b_ref[...], preferred_element_type=F32).astype(BF16)
    for c in range(keys_ref.shape[0]):
        rows = slice(c * PEER_N_KEYS, (c + 1) * PEER_N_KEYS)
        st_ref[rows, :] = lax.dot_general(keys_ref[c], q[:, rows], _NT,
                                          preferred_element_type=F32)


def peer_scores(h, w_q, keys, *, tm, tn):
    t, k = h.shape
    n = w_q.shape[1]
    kb = tn // PEER_N_KEYS
    return pl.pallas_call(
        _peer_scores_kernel,
        grid=(t // tm, n // tn),
        in_specs=[pl.BlockSpec((tm, k), lambda i, j: (i, 0)),
                  pl.BlockSpec((k, tn), lambda i, j: (0, j)),
                  pl.BlockSpec((kb,) + keys.shape[1:], lambda i, j: (j, 0, 0))],
        out_specs=pl.BlockSpec((tn, tm), lambda i, j: (j, i)),
        out_shape=jax.ShapeDtypeStruct((n, t), F32),
        compiler_params=_params("parallel", "arbitrary"),
        name="peer_scores",
    )(h, w_q, keys)


_CAND_PAIRS = [(a, b) for a in range(PEER_TOPK + 1) for b in range(PEER_TOPK + 1)
               if (a + 1) * (b + 1) <= PEER_TOPK + 1]


def _sorting_network(n):
    pairs = []
    p = 1
    while p < n:
        k = p
        while k >= 1:
            for j in range(k % p, n - k, 2 * k):
                for i in range(min(k, n - j - k)):
                    if (i + j) // (2 * p) == (i + j + k) // (2 * p):
                        pairs.append((i + j, i + j + k))
            k //= 2
        p *= 2
    return pairs


def _top_values(s, count):
    groups = s.shape[0] // SUBLANES
    v = [s[g * SUBLANES:(g + 1) * SUBLANES] for g in range(groups)]
    for a, b in _sorting_network(groups):
        v[a], v[b] = jnp.maximum(v[a], v[b]), jnp.minimum(v[a], v[b])
    rows = []
    for k in range(count):
        m = jnp.max(v[0], axis=0, keepdims=True)
        rows.append(m)
        live = min(groups, count - 1 - k)
        if live:
            held = v[0] == m
            for g in range(live):
                v[g] = jnp.where(held, v[g + 1] if g + 1 < groups else NEG, v[g])
    return rows


def _peer_topk_kernel(s_ref, side_ref, e2_ref, thr_ref, hc_ref, side_out_ref, thr_scr, hc_scr):
    nh, _, nk, _ = s_ref.shape
    side_out_ref[...] = side_ref[...].astype(side_out_ref.dtype)

    def head(h, carry):
        s1 = s_ref[h, 0]
        s2 = s_ref[h, 1]
        t1 = _top_values(s1, PEER_TOPK + 1)
        t2 = _top_values(s2, PEER_TOPK + 1)
        pad_rows = pl.next_power_of_2(pl.cdiv(len(_CAND_PAIRS), SUBLANES)) * SUBLANES
        cands = jnp.concatenate(
            [t1[a] + t2[b] for a, b in _CAND_PAIRS]
            + [jnp.full((pad_rows - len(_CAND_PAIRS), s1.shape[1]), NEG, F32)], axis=0)
        top = _top_values(cands, PEER_TOPK + 1)
        tau = 0.5 * (top[PEER_TOPK - 1] + top[PEER_TOPK])
        z = jnp.sum(jnp.where(cands > tau, jnp.exp(cands - top[0]), 0.0), axis=0, keepdims=True)
        thr_scr[h] = tau - s1
        hc_scr[h] = (0.5 * jnp.exp(s1 - t1[0])) / z
        e2_ref[h] = pltpu.bitcast(jnp.exp(s2 - t2[0]).astype(BF16), jnp.uint32)
        return carry

    lax.fori_loop(0, nh, head, 0)
    for h in range(nh):
        for lc in range(thr_ref.shape[0]):
            ls = slice(lc * LANES, (lc + 1) * LANES)
            thr_ref[lc, pl.ds(h, nk, stride=nh), :] = thr_scr[h, :, ls]
            hc_ref[lc, pl.ds(h, nk, stride=nh), :] = hc_scr[h, :, ls]


def peer_topk(st, side, *, tt):
    nh, _, nk, t = st.shape
    assert side.shape[0] % (t // tt) == 0
    side_spec = pl.BlockSpec((side.shape[0] // (t // tt), side.shape[1]), lambda i: (i, 0))
    key_major = pl.BlockSpec((tt // LANES, nk * nh, LANES), lambda i: (i, 0, 0))
    key_major_shape = jax.ShapeDtypeStruct((t // LANES, nk * nh, LANES), F32)
    return pl.pallas_call(
        _peer_topk_kernel,
        grid=(t // tt,),
        in_specs=[pl.BlockSpec((nh, 2, nk, tt), lambda i: (0, 0, 0, i)), side_spec],
        out_specs=[pl.BlockSpec((nh, nk // 2, tt), lambda i: (0, 0, i)), key_major, key_major,
                   side_spec],
        out_shape=[jax.ShapeDtypeStruct((nh, nk // 2, t), jnp.uint32), key_major_shape,
                   key_major_shape, jax.ShapeDtypeStruct(side.shape, BF16)],
        scratch_shapes=[pltpu.VMEM((nh, nk, tt), F32), pltpu.VMEM((nh, nk, tt), F32)],
        compiler_params=_params("parallel"),
        name="peer_topk",
    )(st, side)


def _peer_dense_kernel(h_ref, u_ref, v_ref, s2_ref, e2_ref, thr_ref, hc_ref, x_ref,
                       o_ref, act_scr, w_scr):
    te, tt = act_scr.shape
    nh, nk = s2_ref.shape[0], s2_ref.shape[1]

    @pl.when(pl.program_id(1) == 0)
    def _():
        o_ref[...] = x_ref[...]

    act_scr[...] = lax.dot_general(u_ref[...], h_ref[...], _NT, preferred_element_type=F32)
    for ii in range(te // nk):
        for lc in range(tt // LANES):
            ls = slice(lc * LANES, (lc + 1) * LANES)
            half_gate = None
            for h in range(nh):
                r = ii * nh + h
                half_c = jnp.broadcast_to(hc_ref[lc, r:r + 1, :], (nk, LANES)).astype(BF16)
                weight = pltpu.bitcast(e2_ref[h, :, ls], BF16) * half_c
                part = jnp.where(s2_ref[h, :, ls] >= thr_ref[lc, r:r + 1, :], weight,
                                 jnp.zeros_like(weight))
                half_gate = part if half_gate is None else half_gate + part
            a = act_scr[ii * nk:(ii + 1) * nk, ls]
            t = jnp.tanh(a * (_GELU_C0 + _GELU_C1 * (a * a)))
            w_tile = (half_gate * a) * (1.0 + t)
            w_scr[ls, ii * nk:(ii + 1) * nk] = w_tile.T.astype(w_scr.dtype)
    o_ref[...] += jnp.dot(w_scr[...], v_ref[...], preferred_element_type=F32)


def peer_dense(h, u, v, st, e2, thr, half_c, x, *, tt, te):
    t, d = h.shape
    e = u.shape[0]
    nh, _, nk, _ = st.shape
    once = pl.Buffered(1)
    key_rows = te // nk * nh
    return pl.pallas_call(
        _peer_dense_kernel,
        grid=(t // tt, e // te),
        in_specs=[pl.BlockSpec((tt, d), lambda i, j: (i, 0), pipeline_mode=once),
                  pl.BlockSpec((te, d), lambda i, j: (j, 0)),
                  pl.BlockSpec((te, d), lambda i, j: (j, 0)),
                  pl.BlockSpec((nh, None, nk, tt), lambda i, j: (0, 1, 0, i), pipeline_mode=once),
                  pl.BlockSpec((nh, nk // 2, tt), lambda i, j: (0, 0, i), pipeline_mode=once),
                  pl.BlockSpec((tt // LANES, key_rows, LANES), lambda i, j: (i, j, 0)),
                  pl.BlockSpec((tt // LANES, key_rows, LANES), lambda i, j: (i, j, 0)),
                  pl.BlockSpec((tt, d), lambda i, j: (i, 0))],
        out_specs=pl.BlockSpec((tt, d), lambda i, j: (i, 0)),
        out_shape=jax.ShapeDtypeStruct((t, d), F32),
        scratch_shapes=[pltpu.VMEM((te, tt), F32), pltpu.VMEM((tt, te), BF16)],
        compiler_params=_params("parallel", "arbitrary"),
        name="peer_dense",
    )(h, u, v, st, e2, thr, half_c, x)


def hybrid_layer(x, g_norm1, w_in, g_q_norm, g_k_norm, rel_bias, conv_w, conv_b,
                 w_rg_a, b_rg_a, w_rg_x, b_rg_x, rg_lambda, g_attn_out, g_rec_out,
                 w_out, g_norm2, w_peer_q, peer_keys_1, peer_keys_2, peer_u, peer_v,
                 *, cfg):
    b, s, d = x.shape
    t = b * s
    d_attn = g_attn_out.shape[0]
    d_rec = g_rec_out.shape[0]
    n_heads = d_attn // HEAD_DIM
    x2 = x.reshape(t, d)

    h1 = rmsnorm_rows(x2, g_norm1, cfg["norm_tm"])
    w_in_b = w_in.astype(BF16)
    qkv = in_proj(h1, w_in_b, jnp.stack([g_q_norm, g_k_norm]), col0=0, n_cols=3 * d_attn,
                  n_norm_cols=2 * d_attn, out_dtype=BF16, tm=cfg["mm_tm"], tn=cfg["mm_tn"])
    xy = in_proj(h1, w_in_b, jnp.ones((1, HEAD_DIM), F32), col0=3 * d_attn, n_cols=2 * d_rec,
                 n_norm_cols=0, out_dtype=F32, tm=cfg["mm_tm"], tn=cfg["mm_tn"])

    attn_n = band_attention(qkv, rel_bias, g_attn_out, seq=s, n_heads=n_heads,
                            tq=cfg["attn_tq"])
    rec_n, peer_u_b = rec_branch(xy, conv_w, conv_b, w_rg_a, b_rg_a, w_rg_x, b_rg_x, rg_lambda,
                                 g_rec_out, peer_u, batch=b, tt=cfg["rec_tt"],
                                 lane_chunk=cfg["rec_lanes"])
    x1 = out_proj(attn_n, rec_n, w_out.astype(BF16), x2, tm=cfg["mm_tm"], tn=cfg["mm_tn"])

    h2 = rmsnorm_rows(x1, g_norm2, cfg["norm_tm"])
    nh, nk, half = peer_keys_1.shape
    keys = jnp.stack([peer_keys_1, peer_keys_2], axis=1).reshape(2 * nh, nk, half).astype(BF16)
    st = peer_scores(h2, w_peer_q.astype(BF16), keys, tm=cfg["mm_tm"], tn=cfg["mm_tn"])
    st = st.reshape(nh, 2, nk, t)
    e2, thr, half_c, peer_v_b = peer_topk(st, peer_v, tt=cfg["topk_tt"])
    out = peer_dense(h2, peer_u_b, peer_v_b, st, e2, thr, half_c, x1,
                     tt=cfg["peer_tt"], te=cfg["peer_te"])
    return out.reshape(b, s, d)


_CFG = dict(norm_tm=256, mm_tm=1024, mm_tn=1024, attn_tq=256, rec_tt=256, rec_lanes=512,
            topk_tt=256, peer_tt=512, peer_te=512)


def kernel(x, g_norm1, w_in, g_q_norm, g_k_norm, rel_bias, conv_w, conv_b, w_rg_a, b_rg_a,
           w_rg_x, b_rg_x, rg_lambda, g_attn_out, g_rec_out, w_out, g_norm2, w_peer_q,
           peer_keys_1, peer_keys_2, peer_u, peer_v):
    for l in range(g_norm1.shape[0]):
        x = hybrid_layer(x, g_norm1[l], w_in[l], g_q_norm[l], g_k_norm[l], rel_bias[l],
                         conv_w[l], conv_b[l], w_rg_a[l], b_rg_a[l], w_rg_x[l], b_rg_x[l],
                         rg_lambda[l], g_attn_out[l], g_rec_out[l], w_out[l], g_norm2[l],
                         w_peer_q[l], peer_keys_1[l], peer_keys_2[l], peer_u[l], peer_v[l],
                         cfg=_CFG)
    return x
```

```python
import functools
import math

import jax
import jax.numpy as jnp
from jax import lax
from jax.experimental import pallas as pl
from jax.experimental.pallas import tpu as pltpu

EPS = 1e-6
CHUNK = 64
LEFT_CHUNKS = 8
LEFT = LEFT_CHUNKS * CHUNK
MAX_REL = 256
HEAD_DIM = 128
REC_BLOCK_DIM = 128
CONV_WIDTH = 4
RG_C = 8.0
PEER_HEADS = 8
PEER_N_KEYS = 128
PEER_TOPK = 16
LANES = 128
SUBLANES = 8
NEG = -1e30
LOG2_E = math.log2(math.e)
VMEM_LIMIT_BYTES = 60000 * 1024

F32 = jnp.float32
BF16 = jnp.bfloat16
_NT = (((1,), (1,)), ((), ()))


def _params(*sem):
    return pltpu.CompilerParams(dimension_semantics=sem, vmem_limit_bytes=VMEM_LIMIT_BYTES)


_GELU_C0 = math.sqrt(2.0 / math.pi)
_GELU_C1 = 0.044715 * _GELU_C0


def _gelu_tanh(x):
    return (0.5 * x) * (1.0 + jnp.tanh(x * (_GELU_C0 + _GELU_C1 * (x * x))))


def _sigmoid(x):
    return 0.5 + 0.5 * jnp.tanh(0.5 * x)


def _rmsnorm_kernel(x_ref, g_ref, o_ref):
    x = x_ref[...].astype(F32)
    ms = jnp.mean(x * x, axis=-1, keepdims=True)
    o_ref[...] = (x * lax.rsqrt(ms + EPS) * g_ref[...]).astype(o_ref.dtype)


def rmsnorm_rows(x, g, tm):
    t, d = x.shape
    return pl.pallas_call(
        _rmsnorm_kernel,
        grid=(t // tm,),
        in_specs=[pl.BlockSpec((tm, d), lambda i: (i, 0)),
                  pl.BlockSpec((1, d), lambda i: (0, 0))],
        out_specs=pl.BlockSpec((tm, d), lambda i: (i, 0)),
        out_shape=jax.ShapeDtypeStruct((t, d), BF16),
        compiler_params=_params("parallel"),
        name="rmsnorm_rows",
    )(x, g.reshape(1, d).astype(F32))


def _in_proj_kernel(a_ref, b_ref, g_ref, o_ref, *, n_norm_tiles):
    acc = jnp.dot(a_ref[...], b_ref[...], preferred_element_type=F32)
    j = pl.program_id(1)
    tn = o_ref.shape[1]

    @pl.when(j < n_norm_tiles)
    def _():
        g = g_ref[0]
        for c in range(tn // HEAD_DIM):
            cols = slice(c * HEAD_DIM, (c + 1) * HEAD_DIM)
            blk = acc[:, cols]
            ms = jnp.mean(blk * blk, axis=-1, keepdims=True)
            o_ref[:, cols] = (blk * lax.rsqrt(ms + EPS) * g).astype(o_ref.dtype)

    @pl.when(j >= n_norm_tiles)
    def _():
        o_ref[...] = acc.astype(o_ref.dtype)


def in_proj(h, w, gains, *, col0, n_cols, n_norm_cols, out_dtype, tm, tn):
    t, k = h.shape
    n_sec = gains.shape[0]
    sec_tiles = max(n_norm_cols // tn // n_sec, 1)
    jb0 = col0 // tn
    return pl.pallas_call(
        functools.partial(_in_proj_kernel, n_norm_tiles=n_norm_cols // tn),
        grid=(t // tm, n_cols // tn),
        in_specs=[pl.BlockSpec((tm, k), lambda i, j: (i, 0)),
                  pl.BlockSpec((k, tn), lambda i, j: (0, j + jb0)),
                  pl.BlockSpec((1, 1, HEAD_DIM),
                               lambda i, j: (jnp.minimum(j // sec_tiles, n_sec - 1), 0, 0))],
        out_specs=pl.BlockSpec((tm, tn), lambda i, j: (i, j)),
        out_shape=jax.ShapeDtypeStruct((t, n_cols), out_dtype),
        compiler_params=_params("parallel", "arbitrary"),
        name="in_proj",
    )(h, w, gains.reshape(n_sec, 1, HEAD_DIM).astype(F32))


def _rel_bias_diagonals(rel_bias, tq):
    w = LEFT + tq
    length = pl.cdiv(w + tq - 1, LANES) * LANES
    m = jnp.arange(length)
    d = jnp.where(m < w, m, m - length)
    rel = jnp.clip(LEFT - d, -MAX_REL, MAX_REL) + MAX_REL
    return rel_bias.astype(F32)[:, None, rel]


def _attn_kernel(*refs, n_prev, blocks_per_seq, scale):
    q_ref = refs[0]
    k_refs = refs[1:2 + n_prev]
    v_refs = refs[2 + n_prev:3 + 2 * n_prev]
    diag_ref, g_ref, s1_ref, s2_ref, o_ref, s1_out, s2_out, tbl_scr, o_scr = refs[3 + 2 * n_prev:]
    s1_out[...] = s1_ref[...].astype(s1_out.dtype)
    s2_out[...] = s2_ref[...].astype(s2_out.dtype)
    heads, tq, w = tbl_scr.shape
    i = pl.program_id(0)

    @pl.when(i == 0)
    def _():
        r = lax.broadcasted_iota(jnp.int32, (tq, w), 0)
        c = lax.broadcasted_iota(jnp.int32, (tq, w), 1)
        dchunk = r // CHUNK - c // CHUNK + LEFT_CHUNKS
        band = (dchunk >= 0) & (dchunk <= LEFT_CHUNKS)
        for hh in range(heads):
            diag = jnp.broadcast_to(diag_ref[hh], (tq, diag_ref.shape[2]))
            bias = pltpu.roll(diag, 0, axis=1, stride=1, stride_axis=0)[:, :w]
            tbl_scr[hh] = jnp.where(band, bias * LOG2_E, NEG)

    def heads_loop(mask_fn):
        for hh in range(heads):
            cols = slice(hh * HEAD_DIM, (hh + 1) * HEAD_DIM)
            k = jnp.concatenate([r[:, cols] for r in k_refs], axis=0)
            v = jnp.concatenate([r[:, cols] for r in v_refs], axis=0)
            s = lax.dot_general(q_ref[:, cols], k, _NT, preferred_element_type=F32)
            s = mask_fn(s * (scale * LOG2_E) + tbl_scr[hh])
            m = jnp.max(s, axis=-1, keepdims=True)
            p = jnp.exp2(s - m)
            l = jnp.sum(p, axis=-1, keepdims=True)
            o = jnp.dot(p.astype(v.dtype), v, preferred_element_type=F32)
            o_scr[:, cols] = o / l

    jb = i % blocks_per_seq

    @pl.when(jb < n_prev)
    def _():
        col = lax.broadcasted_iota(jnp.int32, (tq, w), 1)
        in_seq = col >= (n_prev - jb) * tq
        heads_loop(lambda s: jnp.where(in_seq, s, NEG))

    @pl.when(jb >= n_prev)
    def _():
        heads_loop(lambda s: s)

    attn = o_scr[...]
    ms = jnp.mean(attn * attn, axis=-1, keepdims=True)
    o_ref[...] = (attn * lax.rsqrt(ms + EPS) * g_ref[...]).astype(o_ref.dtype)


def band_attention(qkv, rel_bias, g_out, side1, side2, *, seq, n_heads, tq):
    t = qkv.shape[0]
    steps = t // tq
    assert side1.shape[0] % steps == 0 and side2.shape[0] % steps == 0
    side_specs = [pl.BlockSpec((sd.shape[0] // steps, sd.shape[1]), lambda i: (i, 0))
                  for sd in (side1, side2)]
    n_prev = LEFT // tq
    bps = seq // tq
    d_attn = n_heads * HEAD_DIM
    diags = _rel_bias_diagonals(rel_bias, tq)

    def win_spec(col_block, back):
        def idx(i):
            first = (i // bps) * bps
            return (jnp.maximum(i - back, first), col_block)
        return pl.BlockSpec((tq, d_attn), idx)

    backs = list(range(n_prev, -1, -1))
    in_specs = ([pl.BlockSpec((tq, d_attn), lambda i: (i, 0))]
                + [win_spec(1, b) for b in backs]
                + [win_spec(2, b) for b in backs]
                + [pl.BlockSpec(diags.shape, lambda i: (0, 0, 0)),
                   pl.BlockSpec((1, d_attn), lambda i: (0, 0))]
                + side_specs)
    n_win = n_prev + 1
    return pl.pallas_call(
        functools.partial(_attn_kernel, n_prev=n_prev, blocks_per_seq=bps,
                          scale=HEAD_DIM ** -0.5),
        grid=(t // tq,),
        in_specs=in_specs,
        out_specs=[pl.BlockSpec((tq, d_attn), lambda i: (i, 0))] + side_specs,
        out_shape=[jax.ShapeDtypeStruct((t, d_attn), BF16),
                   jax.ShapeDtypeStruct(side1.shape, BF16),
                   jax.ShapeDtypeStruct(side2.shape, BF16)],
        scratch_shapes=[pltpu.VMEM((n_heads, tq, LEFT + tq), F32),
                        pltpu.VMEM((tq, d_attn), F32)],
        compiler_params=_params("arbitrary"),
        name="band_attention",
    )(*([qkv] * (1 + 2 * n_win)), diags, g_out.reshape(1, d_attn).astype(F32), side1, side2)


def _rec_kernel(x_ref, y_ref, cw_ref, cb_ref, wa_ref, wx_ref, ba_ref, bx_ref, lam_ref, g_ref,
                side_ref, o_ref, side_out_ref, xpad, hcar, rec_scr, *, lane_chunk):
    side_out_ref[...] = side_ref[...].astype(side_out_ref.dtype)
    tt, c_all = x_ref.shape
    groups = tt // SUBLANES
    pad0 = SUBLANES - (CONV_WIDTH - 1)

    @pl.when(pl.program_id(1) == 0)
    def _():
        xpad[0:SUBLANES, :] = jnp.zeros((SUBLANES, c_all), F32)
        hcar[...] = jnp.zeros_like(hcar)

    xpad[SUBLANES:SUBLANES + tt, :] = x_ref[...]
    ss = jnp.zeros((tt, 1), F32)
    row = lax.broadcasted_iota(jnp.int32, (groups, SUBLANES, lane_chunk), 1)
    for cbk in range(c_all // lane_chunk):
        ls = slice(cbk * lane_chunk, (cbk + 1) * lane_chunk)
        xr = cb_ref[:, ls]
        for tap in range(CONV_WIDTH):
            xr = xr + xpad[pad0 + tap:pad0 + tap + tt, ls] * cw_ref[tap:tap + 1, ls]
        xr_b = xr.astype(BF16)
        pre_a, pre_x = [], []
        for n in range(lane_chunk // REC_BLOCK_DIM):
            nb = cbk * (lane_chunk // REC_BLOCK_DIM) + n
            xs = xr_b[:, n * REC_BLOCK_DIM:(n + 1) * REC_BLOCK_DIM]
            pre_a.append(jnp.dot(xs, wa_ref[nb], preferred_element_type=F32))
            pre_x.append(jnp.dot(xs, wx_ref[nb], preferred_element_type=F32))
        r = _sigmoid(jnp.concatenate(pre_a, axis=1) + ba_ref[:, ls])
        gate_i = _sigmoid(jnp.concatenate(pre_x, axis=1) + bx_ref[:, ls])
        z = -lam_ref[:, ls]
        softplus = jnp.maximum(z, 0.0) + jnp.log1p(jnp.exp(-jnp.abs(z)))
        log_a = (-RG_C * r) * softplus
        a = jnp.exp(log_a)
        m2 = -jnp.tanh(log_a) * (a * a + 1.0)
        mult = jnp.where(m2 > 0.0, m2 * lax.rsqrt(m2), 0.0)
        u = mult * (gate_i * xr)
        a3 = a.reshape(groups, SUBLANES, lane_chunk)
        u3 = u.reshape(groups, SUBLANES, lane_chunk)
        d = 1
        while d < SUBLANES:
            keep = row >= d
            a_sh = jnp.where(keep, pltpu.roll(a3, d, axis=1), 1.0)
            u_sh = jnp.where(keep, pltpu.roll(u3, d, axis=1), 0.0)
            u3 = a3 * u_sh + u3
            a3 = a3 * a_sh
            d *= 2
        h = hcar[:, ls]
        hs = []
        for gi in range(groups):
            hg = u3[gi] + a3[gi] * h
            hs.append(hg)
            h = hg[SUBLANES - 1:SUBLANES, :]
        hcar[:, ls] = h
        rec = jnp.concatenate(hs, axis=0) * _gelu_tanh(y_ref[:, ls])
        rec_scr[:, ls] = rec
        ss = ss + jnp.sum(rec * rec, axis=-1, keepdims=True)
    xpad[0:SUBLANES, :] = x_ref[tt - SUBLANES:tt, :]
    o_ref[...] = (rec_scr[...] * lax.rsqrt(ss / c_all + EPS) * g_ref[...]).astype(o_ref.dtype)


def rec_branch(xy, conv_w, conv_b, w_a, b_a, w_x, b_x, lam, g_out, side, *, batch, tt,
               lane_chunk):
    t, c2 = xy.shape
    c = c2 // 2
    nt = t // batch // tt
    assert side.shape[0] % (batch * nt) == 0
    side_spec = pl.BlockSpec((side.shape[0] // (batch * nt), side.shape[1]),
                             lambda b, i: (b * nt + i, 0))
    row = lambda v: v.reshape(1, c).astype(F32)
    vec = pl.BlockSpec((1, c), lambda b, i: (0, 0))
    wspec = pl.BlockSpec(w_a.shape, lambda b, i: (0, 0, 0))
    return pl.pallas_call(
        functools.partial(_rec_kernel, lane_chunk=lane_chunk),
        grid=(batch, nt),
        in_specs=[pl.BlockSpec((tt, c), lambda b, i: (b * nt + i, 0)),
                  pl.BlockSpec((tt, c), lambda b, i: (b * nt + i, 1)),
                  pl.BlockSpec((CONV_WIDTH, c), lambda b, i: (0, 0)),
                  vec, wspec, wspec, vec, vec, vec, vec, side_spec],
        out_specs=[pl.BlockSpec((tt, c), lambda b, i: (b * nt + i, 0)), side_spec],
        out_shape=[jax.ShapeDtypeStruct((t, c), BF16), jax.ShapeDtypeStruct(side.shape, BF16)],
        scratch_shapes=[pltpu.VMEM((SUBLANES + tt, c), F32),
                        pltpu.VMEM((1, c), F32),
                        pltpu.VMEM((tt, c), F32)],
        compiler_params=_params("parallel", "arbitrary"),
        name="rec_branch",
    )(xy, xy, conv_w.astype(F32), row(conv_b), w_a.astype(BF16), w_x.astype(BF16),
      row(b_a), row(b_x), row(lam), row(g_out), side)


def _out_proj_kernel(a1_ref, a2_ref, b1_ref, b2_ref, x_ref, o_ref):
    acc = jnp.dot(a1_ref[...], b1_ref[...], preferred_element_type=F32)
    acc = acc + jnp.dot(a2_ref[...], b2_ref[...], preferred_element_type=F32)
    o_ref[...] = x_ref[...] + acc


def out_proj(a1, a2, w, x, *, tm, tn):
    t, k1 = a1.shape
    n = w.shape[1]
    return pl.pallas_call(
        _out_proj_kernel,
        grid=(t // tm, n // tn),
        in_specs=[pl.BlockSpec((tm, k1), lambda i, j: (i, 0)),
                  pl.BlockSpec((tm, k1), lambda i, j: (i, 0)),
                  pl.BlockSpec((k1, tn), lambda i, j: (0, j)),
                  pl.BlockSpec((k1, tn), lambda i, j: (1, j)),
                  pl.BlockSpec((tm, tn), lambda i, j: (i, j))],
        out_specs=pl.BlockSpec((tm, tn), lambda i, j: (i, j)),
        out_shape=jax.ShapeDtypeStruct((t, n), F32),
        compiler_params=_params("parallel", "arbitrary"),
        name="out_proj",
    )(a1, a2, w, w, x)


def _peer_scores_kernel(a_ref, b_ref, keys_ref, st_ref):
    q = jnp.dot(a_ref[...], b_ref[...], preferred_element_type=F32).astype(BF16)
    for c in range(keys_ref.shape[0]):
        rows = slice(c * PEER_N_KEYS, (c + 1) * PEER_N_KEYS)
        st_ref[rows, :] = lax.dot_general(keys_ref[c], q[:, rows], _NT,
                                          preferred_element_type=F32)


def peer_scores(h, w_q, keys, *, tm, tn):
    t, k = h.shape
    n = w_q.shape[1]
    kb = tn // PEER_N_KEYS
    return pl.pallas_call(
        _peer_scores_kernel,
        grid=(t // tm, n // tn),
        in_specs=[pl.BlockSpec((tm, k), lambda i, j: (i, 0)),
                  pl.BlockSpec((k, tn), lambda i, j: (0, j)),
                  pl.BlockSpec((kb,) + keys.shape[1:], lambda i, j: (j, 0, 0))],
        out_specs=pl.BlockSpec((tn, tm), lambda i, j: (j, i)),
        out_shape=jax.ShapeDtypeStruct((n, t), F32),
        compiler_params=_params("parallel", "arbitrary"),
        name="peer_scores",
    )(h, w_q, keys)


_CAND_PAIRS = [(a, b) for a in range(PEER_TOPK + 1) for b in range(PEER_TOPK + 1)
               if (a + 1) * (b + 1) <= PEER_TOPK + 1]


def _sorting_network(n):
    pairs = []
    p = 1
    while p < n:
        k = p
        while k >= 1:
            for j in range(k % p, n - k, 2 * k):
                for i in range(min(k, n - j - k)):
                    if (i + j) // (2 * p) == (i + j + k) // (2 * p):
                        pairs.append((i + j, i + j + k))
            k //= 2
        p *= 2
    return pairs


def _top_values(s, count):
    groups = s.shape[0] // SUBLANES
    v = [s[g * SUBLANES:(g + 1) * SUBLANES] for g in range(groups)]
    for a, b in _sorting_network(groups):
        v[a], v[b] = jnp.maximum(v[a], v[b]), jnp.minimum(v[a], v[b])
    rows = []
    for k in range(count):
        m = jnp.max(v[0], axis=0, keepdims=True)
        rows.append(m)
        live = min(groups, count - 1 - k)
        if live:
            held = v[0] == m
            for g in range(live):
                v[g] = jnp.where(held, v[g + 1] if g + 1 < groups else NEG, v[g])
    return rows


def _peer_topk_kernel(s_ref, side_ref, e2_ref, thr_ref, hc_ref, side_out_ref, thr_scr, hc_scr):
    nh, _, nk, _ = s_ref.shape
    side_out_ref[...] = side_ref[...].astype(side_out_ref.dtype)

    def head(h, carry):
        s1 = s_ref[h, 0]
        s2 = s_ref[h, 1]
        t1 = _top_values(s1, PEER_TOPK + 1)
        t2 = _top_values(s2, PEER_TOPK + 1)
        pad_rows = pl.next_power_of_2(pl.cdiv(len(_CAND_PAIRS), SUBLANES)) * SUBLANES
        cands = jnp.concatenate(
            [t1[a] + t2[b] for a, b in _CAND_PAIRS]
            + [jnp.full((pad_rows - len(_CAND_PAIRS), s1.shape[1]), NEG, F32)], axis=0)
        top = _top_values(cands, PEER_TOPK + 1)
        tau = 0.5 * (top[PEER_TOPK - 1] + top[PEER_TOPK])
        z = jnp.sum(jnp.where(cands > tau, jnp.exp(cands - top[0]), 0.0), axis=0, keepdims=True)
        thr_scr[h] = tau - s1
        hc_scr[h] = (0.5 * jnp.exp(s1 - t1[0])) / z
        e2_ref[h] = pltpu.bitcast(jnp.exp(s2 - t2[0]).astype(BF16), jnp.uint32)
        return carry

    lax.fori_loop(0, nh, head, 0)
    for h in range(nh):
        for lc in range(thr_ref.shape[0]):
            ls = slice(lc * LANES, (lc + 1) * LANES)
            thr_ref[lc, pl.ds(h, nk, stride=nh), :] = thr_scr[h, :, ls]
            hc_ref[lc, pl.ds(h, nk, stride=nh), :] = hc_scr[h, :, ls]


def peer_topk(st, side, *, tt):
    nh, _, nk, t = st.shape
    assert side.shape[0] % (t // tt) == 0
    side_spec = pl.BlockSpec((side.shape[0] // (t // tt), side.shape[1]), lambda i: (i, 0))
    key_major = pl.BlockSpec((tt // LANES, nk * nh, LANES), lambda i: (i, 0, 0))
    key_major_shape = jax.ShapeDtypeStruct((t // LANES, nk * nh, LANES), F32)
    return pl.pallas_call(
        _peer_topk_kernel,
        grid=(t // tt,),
        in_specs=[pl.BlockSpec((nh, 2, nk, tt), lambda i: (0, 0, 0, i)), side_spec],
        out_specs=[pl.BlockSpec((nh, nk // 2, tt), lambda i: (0, 0, i)), key_major, key_major,
                   side_spec],
        out_shape=[jax.ShapeDtypeStruct((nh, nk // 2, t), jnp.uint32), key_major_shape,
                   key_major_shape, jax.ShapeDtypeStruct(side.shape, BF16)],
        scratch_shapes=[pltpu.VMEM((nh, nk, tt), F32), pltpu.VMEM((nh, nk, tt), F32)],
        compiler_params=_params("parallel"),
        name="peer_topk",
    )(st, side)


def _peer_dense_kernel(h_ref, u_ref, v_ref, s2_ref, e2_ref, thr_ref, hc_ref, x_ref,
                       o_ref, act_scr, w_scr):
    te, tt = act_scr.shape
    nh, nk = s2_ref.shape[0], s2_ref.shape[1]

    @pl.when(pl.program_id(1) == 0)
    def _():
        o_ref[...] = x_ref[...]

    act_scr[...] = lax.dot_general(u_ref[...], h_ref[...], _NT, preferred_element_type=F32)
    for ii in range(te // nk):
        for lc in range(tt // LANES):
            ls = slice(lc * LANES, (lc + 1) * LANES)
            half_gate = None
            for h in range(nh):
                r = ii * nh + h
                half_c = jnp.broadcast_to(hc_ref[lc, r:r + 1, :], (nk, LANES)).astype(BF16)
                weight = pltpu.bitcast(e2_ref[h, :, ls], BF16) * half_c
                part = jnp.where(s2_ref[h, :, ls] >= thr_ref[lc, r:r + 1, :], weight,
                                 jnp.zeros_like(weight))
                half_gate = part if half_gate is None else half_gate + part
            a = act_scr[ii * nk:(ii + 1) * nk, ls]
            t = jnp.tanh(a * (_GELU_C0 + _GELU_C1 * (a * a)))
            w_tile = (half_gate * a) * (1.0 + t)
            w_scr[ls, ii * nk:(ii + 1) * nk] = w_tile.T.astype(w_scr.dtype)
    o_ref[...] += jnp.dot(w_scr[...], v_ref[...], preferred_element_type=F32)


def peer_dense(h, u, v, st, e2, thr, half_c, x, *, tt, te):
    t, d = h.shape
    e = u.shape[0]
    nh, _, nk, _ = st.shape
    once = pl.Buffered(1)
    key_rows = te // nk * nh
    return pl.pallas_call(
        _peer_dense_kernel,
        grid=(t // tt, e // te),
        in_specs=[pl.BlockSpec((tt, d), lambda i, j: (i, 0), pipeline_mode=once),
                  pl.BlockSpec((te, d), lambda i, j: (j, 0)),
                  pl.BlockSpec((te, d), lambda i, j: (j, 0)),
                  pl.BlockSpec((nh, None, nk, tt), lambda i, j: (0, 1, 0, i), pipeline_mode=once),
                  pl.BlockSpec((nh, nk // 2, tt), lambda i, j: (0, 0, i), pipeline_mode=once),
                  pl.BlockSpec((tt // LANES, key_rows, LANES), lambda i, j: (i, j, 0)),
                  pl.BlockSpec((tt // LANES, key_rows, LANES), lambda i, j: (i, j, 0)),
                  pl.BlockSpec((tt, d), lambda i, j: (i, 0))],
        out_specs=pl.BlockSpec((tt, d), lambda i, j: (i, 0)),
        out_shape=jax.ShapeDtypeStruct((t, d), F32),
        scratch_shapes=[pltpu.VMEM((te, tt), F32), pltpu.VMEM((tt, te), BF16)],
        compiler_params=_params("parallel", "arbitrary"),
        name="peer_dense",
    )(h, u, v, st, e2, thr, half_c, x)


def hybrid_layer(x, g_norm1, w_in, g_q_norm, g_k_norm, rel_bias, conv_w, conv_b,
                 w_rg_a, b_rg_a, w_rg_x, b_rg_x, rg_lambda, g_attn_out, g_rec_out,
                 w_out, g_norm2, w_peer_q, peer_keys_1, peer_keys_2, peer_u, peer_v,
                 *, cfg):
    b, s, d = x.shape
    t = b * s
    d_attn = g_attn_out.shape[0]
    d_rec = g_rec_out.shape[0]
    n_heads = d_attn // HEAD_DIM
    x2 = x.reshape(t, d)

    h1 = rmsnorm_rows(x2, g_norm1, cfg["norm_tm"])
    w_in_b = w_in.astype(BF16)
    qkv = in_proj(h1, w_in_b, jnp.stack([g_q_norm, g_k_norm]), col0=0, n_cols=3 * d_attn,
                  n_norm_cols=2 * d_attn, out_dtype=BF16, tm=cfg["mm_tm"], tn=cfg["mm_tn"])
    xy = in_proj(h1, w_in_b, jnp.ones((1, HEAD_DIM), F32), col0=3 * d_attn, n_cols=2 * d_rec,
                 n_norm_cols=0, out_dtype=F32, tm=cfg["mm_tm"], tn=cfg["mm_tn"])

    attn_n, w_out_b, w_peer_q_b = band_attention(qkv, rel_bias, g_attn_out, w_out, w_peer_q,
                                                 seq=s, n_heads=n_heads, tq=cfg["attn_tq"])
    rec_n, peer_u_b = rec_branch(xy, conv_w, conv_b, w_rg_a, b_rg_a, w_rg_x, b_rg_x, rg_lambda,
                                 g_rec_out, peer_u, batch=b, tt=cfg["rec_tt"],
                                 lane_chunk=cfg["rec_lanes"])
    x1 = out_proj(attn_n, rec_n, w_out_b, x2, tm=cfg["mm_tm"], tn=cfg["mm_tn"])

    h2 = rmsnorm_rows(x1, g_norm2, cfg["norm_tm"])
    nh, nk, half = peer_keys_1.shape
    keys = jnp.stack([peer_keys_1, peer_keys_2], axis=1).reshape(2 * nh, nk, half).astype(BF16)
    st = peer_scores(h2, w_peer_q_b, keys, tm=cfg["mm_tm"], tn=cfg["mm_tn"])
    st = st.reshape(nh, 2, nk, t)
    e2, thr, half_c, peer_v_b = peer_topk(st, peer_v, tt=cfg["topk_tt"])
    out = peer_dense(h2, peer_u_b, peer_v_b, st, e2, thr, half_c, x1,
                     tt=cfg["peer_tt"], te=cfg["peer_te"])
    return out.reshape(b, s, d)


_CFG = dict(norm_tm=256, mm_tm=1024, mm_tn=1024, attn_tq=256, rec_tt=256, rec_lanes=512,
            topk_tt=256, peer_tt=512, peer_te=512)


def kernel(x, g_norm1, w_in, g_q_norm, g_k_norm, rel_bias, conv_w, conv_b, w_rg_a, b_rg_a,
           w_rg_x, b_rg_x, rg_lambda, g_attn_out, g_rec_out, w_out, g_norm2, w_peer_q,
           peer_keys_1, peer_keys_2, peer_u, peer_v):
    for l in range(g_norm1.shape[0]):
        x = hybrid_layer(x, g_norm1[l], w_in[l], g_q_norm[l], g_k_norm[l], rel_bias[l],
                         conv_w[l], conv_b[l], w_rg_a[l], b_rg_a[l], w_rg_x[l], b_rg_x[l],
                         rg_lambda[l], g_attn_out[l], g_rec_out[l], w_out[l], g_norm2[l],
                         w_peer_q[l], peer_keys_1[l], peer_keys_2[l], peer_u[l], peer_v[l],
                         cfg=_CFG)
    return x
```
